```python
import math
import jax, jax.numpy as jnp
from jax import lax
import numpy as np

D_MODEL = 1024
BATCH = 16
SEQ = 2048
DEPTH = 1

HEAD_DIM = 64
FOX_HEADS = 8
MOBA_HEADS = 8
FOX_WIDTH = FOX_HEADS * HEAD_DIM
MOBA_WIDTH = MOBA_HEADS * HEAD_DIM
MIX_WIDTH = FOX_WIDTH + MOBA_WIDTH
IN_WIDTH = 4 * FOX_WIDTH + FOX_HEADS + 4 * MOBA_WIDTH
FOX_Q_BLOCK = 128
MOBA_BLOCK = 256
MOBA_TOPK = 3
MOBA_Q_CHUNK = 16
RMS_EPS = 1e-6

kernel_name = "hybrid_fox_moba_block"


def rms_norm(x, g):
    xf = x.astype(jnp.float32)
    y = xf * lax.rsqrt(jnp.mean(xf * xf, axis=-1, keepdims=True) + RMS_EPS)
    return (y * g.astype(jnp.float32)).astype(x.dtype)


def alibi_slopes(n_heads):
    return jnp.asarray(2.0 ** (-8.0 * np.arange(1, n_heads + 1) / n_heads), dtype=jnp.float32)


def split_heads(t, n_heads):
    b, s, _ = t.shape
    return t.reshape(b, s, n_heads, HEAD_DIM).transpose(0, 2, 1, 3)


def merge_heads(t):
    b, h, s, d = t.shape
    return t.transpose(0, 2, 1, 3).reshape(b, s, h * d)


def fox_attention(q, k, v, log_f):
    S = q.shape[2]
    F = jnp.cumsum(log_f, axis=-1)
    scale = HEAD_DIM ** -0.5
    outs = []
    for start in range(0, S, FOX_Q_BLOCK):
        end = start + FOX_Q_BLOCK
        qb = q[:, :, start:end]
        kb = k[:, :, :end]
        vb = v[:, :, :end]
        logits = jnp.einsum('bhtd,bhsd->bhts', qb, kb, preferred_element_type=jnp.float32) * scale
        logits = logits + F[:, :, start:end, None] - F[:, :, None, :end]
        t_pos = jnp.arange(start, end)[:, None]
        s_pos = jnp.arange(end)[None, :]
        logits = jnp.where(s_pos <= t_pos, logits, -jnp.inf)
        p = jax.nn.softmax(logits, axis=-1)
        outs.append(jnp.einsum('bhts,bhsd->bhtd', p.astype(v.dtype), vb))
    return jnp.concatenate(outs, axis=2)


def moba_attention(q, k, v, slopes):
    B, H, S, d = q.shape
    nb = -(-S // MOBA_BLOCK)
    pad = nb * MOBA_BLOCK - S
    kp = jnp.pad(k, ((0, 0), (0, 0), (0, pad), (0, 0)))
    vp = jnp.pad(v, ((0, 0), (0, 0), (0, pad), (0, 0)))
    kblk = kp.reshape(B, H, nb, MOBA_BLOCK, d)
    vblk = vp.reshape(B, H, nb, MOBA_BLOCK, d)
    kmean = jnp.mean(kblk.astype(jnp.float32), axis=3)
    n_sel = min(MOBA_TOPK, nb - 1)
    scale = HEAD_DIM ** -0.5
    n_chunks = S // MOBA_Q_CHUNK
    b_idx = jnp.arange(B)[:, None, None, None]
    h_idx = jnp.arange(H)[None, :, None, None]
    blk_ids = jnp.arange(nb)
    in_blk = jnp.arange(MOBA_BLOCK)

    def chunk_fn(c):
        start = c * MOBA_Q_CHUNK
        qc = lax.dynamic_slice_in_dim(q, start, MOBA_Q_CHUNK, axis=2)
        t_pos = start + jnp.arange(MOBA_Q_CHUNK)
        qblk = start // MOBA_BLOCK
        own_start = qblk * MOBA_BLOCK
        k_own = lax.dynamic_slice_in_dim(kp, own_start, MOBA_BLOCK, axis=2)
        v_own = lax.dynamic_slice_in_dim(vp, own_start, MOBA_BLOCK, axis=2)
        s_own = own_start + in_blk
        dist_own = (t_pos[:, None] - s_own[None, :]).astype(jnp.float32)
        lg_own = jnp.einsum('bhqd,bhkd->bhqk', qc, k_own, preferred_element_type=jnp.float32) * scale
        lg_own = lg_own - slopes[:, None, None] * dist_own
        lg_own = jnp.where(s_own[None, :] <= t_pos[:, None], lg_own, -jnp.inf)
        if n_sel == 0:
            p_own = jax.nn.softmax(lg_own, axis=-1)
            return jnp.einsum('bhqk,bhkd->bhqd', p_own.astype(v.dtype), v_own)
        gate = jnp.einsum('bhqd,bhnd->bhqn', qc.astype(jnp.float32), kmean)
        gate = jnp.where(blk_ids < qblk, gate, -jnp.inf)
        _, sel = lax.top_k(gate, n_sel)
        sel_valid = sel < qblk
        k_sel = kblk[b_idx, h_idx, sel]
        v_sel = vblk[b_idx, h_idx, sel]
        s_sel = sel[..., None] * MOBA_BLOCK + in_blk
        dist_sel = (t_pos[None, None, :, None, None] - s_sel).astype(jnp.float32)
        lg_sel = jnp.einsum('bhqd,bhqnkd->bhqnk', qc, k_sel, preferred_element_type=jnp.float32) * scale
        lg_sel = lg_sel - slopes[None, :, None, None, None] * dist_sel
        lg_sel = jnp.where(sel_valid[..., None], lg_sel, -jnp.inf)
        lg_sel = lg_sel.reshape(B, H, MOBA_Q_CHUNK, n_sel * MOBA_BLOCK)
        p = jax.nn.softmax(jnp.concatenate([lg_sel, lg_own], axis=-1), axis=-1).astype(v.dtype)
        p_sel = p[..., :n_sel * MOBA_BLOCK].reshape(B, H, MOBA_Q_CHUNK, n_sel, MOBA_BLOCK)
        p_own = p[..., n_sel * MOBA_BLOCK:]
        return (jnp.einsum('bhqnk,bhqnkd->bhqd', p_sel, v_sel)
                + jnp.einsum('bhqk,bhkd->bhqd', p_own, v_own))

    outs = lax.map(chunk_fn, jnp.arange(n_chunks))
    return outs.transpose(1, 2, 0, 3, 4).reshape(B, H, S, d)


def setup_inputs(seed: int = 0) -> dict:
    key = jax.random.key(seed)
    kx, k1, k2, k3, k4, k5 = jax.random.split(key, 6)
    x = jax.random.normal(kx, (BATCH, SEQ, D_MODEL), jnp.float32)
    norm_pre_g = 1.0 + 0.02 * jax.random.normal(k1, (DEPTH, D_MODEL), jnp.float32)
    w_in = jax.random.normal(k2, (DEPTH, D_MODEL, IN_WIDTH), jnp.float32) * D_MODEL ** -0.5
    fox_forget_b = jax.random.uniform(k3, (DEPTH, FOX_HEADS), jnp.float32, minval=1.0, maxval=5.0)
    w_out = jax.random.normal(k4, (DEPTH, MIX_WIDTH, D_MODEL), jnp.float32) * MIX_WIDTH ** -0.5
    norm_post_g = 1.0 + 0.02 * jax.random.normal(k5, (DEPTH, D_MODEL), jnp.float32)
    return {"x": x, "norm_pre_g": norm_pre_g, "w_in": w_in, "fox_forget_b": fox_forget_b,
            "w_out": w_out, "norm_post_g": norm_post_g}


def reference(x, norm_pre_g, w_in, fox_forget_b, w_out, norm_post_g):
    sizes = [FOX_WIDTH] * 4 + [FOX_HEADS] + [MOBA_WIDTH] * 4
    offsets = [int(o) for o in np.cumsum(sizes)[:-1]]
    slopes = alibi_slopes(MOBA_HEADS)
    for layer in range(DEPTH):
        h = rms_norm(x, norm_pre_g[layer])
        proj = jnp.einsum('bsd,de->bse', h, w_in[layer])
        q_a, k_a, v_a, g_a, f_a, q_b, k_b, v_b, g_b = jnp.split(proj, offsets, axis=-1)
        log_f = jax.nn.log_sigmoid((f_a + fox_forget_b[layer]).astype(jnp.float32))
        y_a = fox_attention(split_heads(q_a, FOX_HEADS), split_heads(k_a, FOX_HEADS),
                            split_heads(v_a, FOX_HEADS), log_f.transpose(0, 2, 1))
        y_a = merge_heads(y_a) * jax.nn.silu(g_a)
        y_b = moba_attention(split_heads(q_b, MOBA_HEADS), split_heads(k_b, MOBA_HEADS),
                             split_heads(v_b, MOBA_HEADS), slopes)
        y_b = merge_heads(y_b) * jax.nn.silu(g_b)
        y = jnp.einsum('bse,ed->bsd', jnp.concatenate([y_a, y_b], axis=-1), w_out[layer])
        x = x + rms_norm(y, norm_post_g[layer])
    return x
```

```python
import functools

import jax
import jax.numpy as jnp
import numpy as np
from jax import lax
from jax.experimental import pallas as pl
from jax.experimental.pallas import tpu as pltpu

D_MODEL = 1024
HEAD_DIM = 64
N_HEADS = 8
GROUP_WIDTH = N_HEADS * HEAD_DIM
MOBA_BLOCK = 256
MOBA_TOPK = 3
RMS_EPS = 1e-6

BQ = 256
BK = 256
K_AUG = 128
MASK_NEG = -1e30
F32 = jnp.float32
BF16 = jnp.bfloat16

VMEM_LIMIT = 56 * 1024 * 1024


def _in_proj_kernel(x_ref, g_ref, w_ref, wf_ref, p_ref, f_ref, *, col_chunk):
    x = x_ref[...]
    ms = jnp.mean(x * x, axis=-1, keepdims=True)
    h = (x * lax.rsqrt(ms + RMS_EPS) * g_ref[...]).astype(BF16)
    n_cols = w_ref.shape[1]
    for c in range(0, n_cols, col_chunk):
        p_ref[:, c:c + col_chunk] = jnp.dot(
            h, w_ref[:, c:c + col_chunk], preferred_element_type=F32).astype(BF16)
    f_ref[...] = jnp.dot(h, wf_ref[...], preferred_element_type=F32)


def _in_proj(x2d, g, w, wf, *, tm=512):
    n, d = x2d.shape
    n_cols = w.shape[1]
    return pl.pallas_call(
        functools.partial(_in_proj_kernel, col_chunk=1024),
        out_shape=(jax.ShapeDtypeStruct((n, n_cols), BF16),
                   jax.ShapeDtypeStruct((n, 128), F32)),
        grid=(n // tm,),
        in_specs=[pl.BlockSpec((tm, d), lambda i: (i, 0)),
                  pl.BlockSpec((1, d), lambda i: (0, 0)),
                  pl.BlockSpec((d, n_cols), lambda i: (0, 0)),
                  pl.BlockSpec((d, 128), lambda i: (0, 0))],
        out_specs=(pl.BlockSpec((tm, n_cols), lambda i: (i, 0)),
                   pl.BlockSpec((tm, 128), lambda i: (i, 0))),
        compiler_params=pltpu.CompilerParams(
            dimension_semantics=("arbitrary",), vmem_limit_bytes=VMEM_LIMIT),
        name="in_proj",
    )(x2d, g, w, wf)


def _split3(v):
    hi = v.astype(BF16)
    r = v - hi.astype(F32)
    mid = r.astype(BF16)
    lo = (r - mid.astype(F32)).astype(BF16)
    return hi, mid, lo


def _fox_cumsum_kernel(f_ref, b_ref, o_ref, *, chunk):
    s = f_ref.shape[1]
    rows = lax.broadcasted_iota(jnp.int32, (chunk, chunk), 0)
    cols = lax.broadcasted_iota(jnp.int32, (chunk, chunk), 1)
    tri = (rows >= cols).astype(F32)
    lane = lax.broadcasted_iota(jnp.int32, (chunk, 128), 1)
    carry = jnp.zeros((1, 128), F32)
    for c in range(0, s, chunk):
        z = f_ref[0, c:c + chunk, :] + b_ref[...]
        log_f = jnp.minimum(z, 0.0) - jnp.log1p(jnp.exp(-jnp.abs(z)))
        cum = jnp.dot(tri, log_f, preferred_element_type=F32,
                      precision=lax.Precision.HIGHEST) + carry
        carry = cum[chunk - 1:chunk, :]
        hi, mid, lo = _split3(cum)
        zero = jnp.zeros_like(hi)
        parts = jnp.where(lane < 8, hi, jnp.where(lane < 16, mid, jnp.where(lane < 24, lo, zero)))
        o_ref[0, c:c + chunk, :] = parts[:, :HEAD_DIM]


def _fox_cumsum(flog, bias_row):
    b, s, _ = flog.shape
    return pl.pallas_call(
        functools.partial(_fox_cumsum_kernel, chunk=256),
        out_shape=jax.ShapeDtypeStruct((b, s, HEAD_DIM), BF16),
        grid=(b,),
        in_specs=[pl.BlockSpec((1, s, 128), lambda i: (i, 0, 0)),
                  pl.BlockSpec((1, 128), lambda i: (0, 0))],
        out_specs=pl.BlockSpec((1, s, HEAD_DIM), lambda i: (i, 0, 0)),
        compiler_params=pltpu.CompilerParams(dimension_semantics=("arbitrary",)),
        name="fox_cumsum",
    )(flog, bias_row)


def _attn_kernel(qt_ref, kaug_ref, vaug_ref, gt_ref, o_ref, qaug_ref, *, moba):
    n_blocks = qt_ref.shape[0]
    head = pl.program_id(1)
    row64 = lax.broadcasted_iota(jnp.int32, (HEAD_DIM, BQ), 0)
    row8 = lax.broadcasted_iota(jnp.int32, (8, BQ), 0)
    if moba:
        qaug_ref[HEAD_DIM:, :] = jnp.where(row64 < 3, 1.0, 0.0).astype(BF16)
        km = jnp.zeros((16, K_AUG), F32)
        row16 = lax.broadcasted_iota(jnp.int32, (16, K_AUG), 0)
        for j in range(n_blocks):
            mean_j = jnp.sum(kaug_ref[j].astype(F32), axis=0, keepdims=True) * (1.0 / BK)
            km = jnp.where(row16 == j, mean_j, km)
        kmean = km[:, :HEAD_DIM].astype(BF16)
    else:
        sel = (row64 == head) | (row64 == head + 8) | (row64 == head + 16)
        qaug_ref[HEAD_DIM:, :] = jnp.where(sel, -1.0, 0.0).astype(BF16)

    key_idx = lax.broadcasted_iota(jnp.int32, (BK, BQ), 0)
    qry_idx = lax.broadcasted_iota(jnp.int32, (BK, BQ), 1)
    causal = key_idx <= qry_idx

    def pair(j, m, acc, diag):
        s = jnp.dot(kaug_ref[j], qaug_ref[...], preferred_element_type=F32)
        if diag:
            s = jnp.where(causal, s, -jnp.inf)
        m_new = jnp.maximum(m, jnp.max(s, axis=0, keepdims=True))
        alpha = jnp.exp(m - m_new)
        p = jnp.exp(s - m_new).astype(BF16)
        acc = alpha * acc + jnp.dot(vaug_ref[j], p, preferred_element_type=F32)
        return m_new, acc

    def q_block(i, carry):
        q_t = qt_ref[i]
        qaug_ref[:HEAD_DIM, :] = q_t
        if moba:
            gate = jnp.dot(kmean, q_t, preferred_element_type=F32)[:8, :]
            gate = jnp.where(row8 < i, gate, -jnp.inf)
            mask = jnp.zeros((8, BQ), F32)
            for j in range(n_blocks):
                g_j = gate[j:j + 1, :]
                beats = (gate > g_j) | ((gate == g_j) & (row8 < j))
                rank = jnp.sum(beats.astype(F32), axis=0, keepdims=True)
                keep = rank < jnp.where(j < i, float(MOBA_TOPK), 0.0)
                dropped = jnp.where(j == i, 0.0, MASK_NEG)
                mask = jnp.where(row8 == j, jnp.where(keep, 0.0, dropped), mask)
            alibi_rows = jnp.where(row8 < 3, 1.0, 0.0)
            qaug_ref[HEAD_DIM:HEAD_DIM + 16, :] = jnp.concatenate(
                [alibi_rows, mask], axis=0).astype(BF16)
        m0 = jnp.full((1, BQ), -jnp.inf, F32)
        acc0 = jnp.zeros((K_AUG, BQ), F32)
        m, acc = pair(i, m0, acc0, True)
        m, acc = lax.fori_loop(0, i, lambda j, c: pair(j, c[0], c[1], False), (m, acc))
        out = acc[:HEAD_DIM, :] / acc[HEAD_DIM:HEAD_DIM + 1, :]
        g = gt_ref[i].astype(F32)
        o_ref[i] = (out * (g / (1.0 + jnp.exp(-g)))).astype(BF16)
        return carry

    lax.fori_loop(0, n_blocks, q_block, 0)


def _attention(qt, kaug, vaug, gt, *, moba):
    b, h, nb = qt.shape[:3]
    spec_q = pl.BlockSpec((None, None, nb, HEAD_DIM, BQ), lambda i, j: (i, j, 0, 0, 0))
    return pl.pallas_call(
        functools.partial(_attn_kernel, moba=moba),
        out_shape=jax.ShapeDtypeStruct((b, h, nb, HEAD_DIM, BQ), BF16),
        grid=(b, h),
        in_specs=[spec_q,
                  pl.BlockSpec((None, None, nb, BK, K_AUG), lambda i, j: (i, j, 0, 0, 0)),
                  pl.BlockSpec((None, None, nb, K_AUG, BK), lambda i, j: (i, j, 0, 0, 0)),
                  spec_q],
        out_specs=spec_q,
        scratch_shapes=[pltpu.VMEM((K_AUG, BQ), BF16)],
        compiler_params=pltpu.CompilerParams(
            dimension_semantics=("arbitrary", "arbitrary"), vmem_limit_bytes=VMEM_LIMIT),
        name="moba_attention" if moba else "fox_attention",
    )(qt, kaug, vaug, gt)


def _out_proj_kernel(y_ref, w_ref, x_ref, g_ref, o_ref):
    y = jnp.dot(y_ref[...], w_ref[...], preferred_element_type=F32)
    ms = jnp.mean(y * y, axis=-1, keepdims=True)
    o_ref[...] = x_ref[...] + y * lax.rsqrt(ms + RMS_EPS) * g_ref[...]


def _out_proj(y2d, w, x2d, g, *, tm=512):
    n, d = x2d.shape
    return pl.pallas_call(
        _out_proj_kernel,
        out_shape=jax.ShapeDtypeStruct((n, d), F32),
        grid=(n // tm,),
        in_specs=[pl.BlockSpec((tm, y2d.shape[1]), lambda i: (i, 0)),
                  pl.BlockSpec(w.shape, lambda i: (0, 0)),
                  pl.BlockSpec((tm, d), lambda i: (i, 0)),
                  pl.BlockSpec((1, d), lambda i: (0, 0))],
        out_specs=pl.BlockSpec((tm, d), lambda i: (i, 0)),
        compiler_params=pltpu.CompilerParams(
            dimension_semantics=("arbitrary",), vmem_limit_bytes=VMEM_LIMIT),
        name="out_proj",
    )(y2d, w, x2d, g)


def _alibi_key_features(s):
    slopes = np.asarray(2.0 ** (-8.0 * np.arange(1, N_HEADS + 1) / N_HEADS), dtype=np.float32)
    bias = slopes[:, None] * np.arange(s, dtype=np.float32)[None, :]

    def top_bits(v):
        return (v.view(np.uint32) & np.uint32(0xFFFF0000)).view(np.float32)

    hi = top_bits(bias)
    mid = top_bits(bias - hi)
    lo = bias - hi - mid
    feats = np.zeros((N_HEADS, s, HEAD_DIM), np.float32)
    feats[:, :, 0], feats[:, :, 1], feats[:, :, 2] = hi, mid, lo
    feats[:, :, 8:16] = (np.arange(s)[:, None] // MOBA_BLOCK == np.arange(8)[None, :])[None]
    return jnp.asarray(feats, dtype=BF16)


def _layer(x, g_pre, w_in, f_b, w_out, g_post):
    b, s, d = x.shape
    nb = s // BQ
    gw = GROUP_WIDTH
    scale = HEAD_DIM ** -0.5
    fo = 4 * gw
    wa, wf, wb = w_in[:, :fo], w_in[:, fo:fo + N_HEADS], w_in[:, fo + N_HEADS:]
    w = jnp.concatenate([wa[:, :gw] * scale, wa[:, gw:], wb[:, :gw] * scale, wb[:, gw:]],
                        axis=1).astype(BF16)
    wf3 = jnp.concatenate([wf, wf, wf, jnp.zeros((d, 128 - 3 * N_HEADS), F32)], axis=1).astype(BF16)
    fb3 = jnp.concatenate([f_b, f_b, f_b, jnp.zeros((128 - 3 * N_HEADS,), F32)])[None, :]

    x2d = x.reshape(b * s, d)
    proj, flog = _in_proj(x2d, g_pre[None, :], w, wf3)
    f_parts = _fox_cumsum(flog.reshape(b, s, 128), fb3)

    p6 = proj.reshape(b, nb, BQ, 8, N_HEADS, HEAD_DIM)

    def t_blocks(part):
        return p6[:, :, :, part].transpose(0, 3, 1, 4, 2)

    def r_blocks(part):
        return p6[:, :, :, part].transpose(0, 3, 1, 2, 4)

    def v_aug(part):
        vt = t_blocks(part)
        ones = jnp.ones((b, N_HEADS, nb, 1, BQ), BF16)
        zeros = jnp.zeros((b, N_HEADS, nb, K_AUG - HEAD_DIM - 1, BQ), BF16)
        return jnp.concatenate([vt, ones, zeros], axis=3)

    shape_k = (b, N_HEADS, nb, BK, HEAD_DIM)
    kaug_a = jnp.concatenate(
        [r_blocks(1), jnp.broadcast_to(f_parts.reshape(b, 1, nb, BK, HEAD_DIM), shape_k)], axis=-1)
    kaug_b = jnp.concatenate(
        [r_blocks(5),
         jnp.broadcast_to(_alibi_key_features(s).reshape(1, N_HEADS, nb, BK, HEAD_DIM), shape_k)],
        axis=-1)

    yt_a = _attention(t_blocks(0), kaug_a, v_aug(2), t_blocks(3), moba=False)
    yt_b = _attention(t_blocks(4), kaug_b, v_aug(6), t_blocks(7), moba=True)

    def merge(yt):
        return yt.transpose(0, 2, 4, 1, 3).reshape(b * s, gw)

    y2d = jnp.concatenate([merge(yt_a), merge(yt_b)], axis=1)
    out = _out_proj(y2d, w_out.astype(BF16), x2d, g_post[None, :])
    return out.reshape(b, s, d)


def kernel(x, norm_pre_g, w_in, fox_forget_b, w_out, norm_post_g):
    for layer in range(norm_pre_g.shape[0]):
        x = _layer(x, norm_pre_g[layer], w_in[layer], fox_forget_b[layer], w_out[layer],
                   norm_post_g[layer])
    return x
```

```python
import functools

import jax
import jax.numpy as jnp
import numpy as np
from jax import lax
from jax.experimental import pallas as pl
from jax.experimental.pallas import tpu as pltpu

D_MODEL = 1024
HEAD_DIM = 64
N_HEADS = 8
GROUP_WIDTH = N_HEADS * HEAD_DIM
MOBA_BLOCK = 256
MOBA_TOPK = 3
RMS_EPS = 1e-6

BQ = 256
BK = 256
K_AUG = 128
MASK_NEG = -1e30
F32 = jnp.float32
BF16 = jnp.bfloat16

VMEM_LIMIT = 56 * 1024 * 1024


def _in_proj_kernel(x_ref, g_ref, w_ref, wf_ref, p_ref, f_ref, *, col_chunk):
    x = x_ref[...]
    ms = jnp.mean(x * x, axis=-1, keepdims=True)
    h = (x * lax.rsqrt(ms + RMS_EPS) * g_ref[...]).astype(BF16)
    n_cols = w_ref.shape[1]
    for c in range(0, n_cols, col_chunk):
        p_ref[:, c:c + col_chunk] = jnp.dot(
            h, w_ref[:, c:c + col_chunk], preferred_element_type=F32).astype(BF16)
    f_ref[...] = jnp.dot(h, wf_ref[...], preferred_element_type=F32)


def _in_proj(x2d, g, w, wf, *, tm=512):
    n, d = x2d.shape
    n_cols = w.shape[1]
    return pl.pallas_call(
        functools.partial(_in_proj_kernel, col_chunk=1024),
        out_shape=(jax.ShapeDtypeStruct((n, n_cols), BF16),
                   jax.ShapeDtypeStruct((n, 128), F32)),
        grid=(n // tm,),
        in_specs=[pl.BlockSpec((tm, d), lambda i: (i, 0)),
                  pl.BlockSpec((1, d), lambda i: (0, 0)),
                  pl.BlockSpec((d, n_cols), lambda i: (0, 0)),
                  pl.BlockSpec((d, 128), lambda i: (0, 0))],
        out_specs=(pl.BlockSpec((tm, n_cols), lambda i: (i, 0)),
                   pl.BlockSpec((tm, 128), lambda i: (i, 0))),
        compiler_params=pltpu.CompilerParams(
            dimension_semantics=("arbitrary",), vmem_limit_bytes=VMEM_LIMIT),
        name="in_proj",
    )(x2d, g, w, wf)


def _split3(v):
    hi = v.astype(BF16)
    r = v - hi.astype(F32)
    mid = r.astype(BF16)
    lo = (r - mid.astype(F32)).astype(BF16)
    return hi, mid, lo


def _fox_cumsum_kernel(f_ref, b_ref, o_ref, *, chunk):
    s = f_ref.shape[1]
    rows = lax.broadcasted_iota(jnp.int32, (chunk, chunk), 0)
    cols = lax.broadcasted_iota(jnp.int32, (chunk, chunk), 1)
    tri = (rows >= cols).astype(F32)
    lane = lax.broadcasted_iota(jnp.int32, (chunk, 128), 1)
    carry = jnp.zeros((1, 128), F32)
    for c in range(0, s, chunk):
        z = f_ref[0, c:c + chunk, :] + b_ref[...]
        log_f = jnp.minimum(z, 0.0) - jnp.log1p(jnp.exp(-jnp.abs(z)))
        cum = jnp.dot(tri, log_f, preferred_element_type=F32,
                      precision=lax.Precision.HIGHEST) + carry
        carry = cum[chunk - 1:chunk, :]
        hi, mid, lo = _split3(cum)
        zero = jnp.zeros_like(hi)
        parts = jnp.where(lane < 8, hi, jnp.where(lane < 16, mid, jnp.where(lane < 24, lo, zero)))
        o_ref[0, c:c + chunk, :] = parts[:, :HEAD_DIM]


def _fox_cumsum(flog, bias_row):
    b, s, _ = flog.shape
    return pl.pallas_call(
        functools.partial(_fox_cumsum_kernel, chunk=256),
        out_shape=jax.ShapeDtypeStruct((b, s, HEAD_DIM), BF16),
        grid=(b,),
        in_specs=[pl.BlockSpec((1, s, 128), lambda i: (i, 0, 0)),
                  pl.BlockSpec((1, 128), lambda i: (0, 0))],
        out_specs=pl.BlockSpec((1, s, HEAD_DIM), lambda i: (i, 0, 0)),
        compiler_params=pltpu.CompilerParams(dimension_semantics=("arbitrary",)),
        name="fox_cumsum",
    )(flog, bias_row)


def _attn_kernel(qt_ref, kaug_ref, vaug_ref, gt_ref, o_ref, qaug_ref, s_ref, p_ref, *, moba):
    n_blocks = qt_ref.shape[0]
    head = pl.program_id(1)
    row64 = lax.broadcasted_iota(jnp.int32, (HEAD_DIM, BQ), 0)
    row8 = lax.broadcasted_iota(jnp.int32, (8, BQ), 0)
    if moba:
        tail = jnp.where(row64 < 3, 1.0, 0.0).astype(BF16)
        km = jnp.zeros((16, K_AUG), F32)
        row16 = lax.broadcasted_iota(jnp.int32, (16, K_AUG), 0)
        for j in range(n_blocks):
            blk = kaug_ref[j * BK:(j + 1) * BK, :].astype(F32)
            km = jnp.where(row16 == j, jnp.sum(blk, axis=0, keepdims=True) * (1.0 / BK), km)
        kmean = km[:, :HEAD_DIM].astype(BF16)
    else:
        sel = (row64 == head) | (row64 == head + 8) | (row64 == head + 16)
        tail = jnp.where(sel, -1.0, 0.0).astype(BF16)
    qaug_ref[0, HEAD_DIM:, :] = tail
    qaug_ref[1, HEAD_DIM:, :] = tail

    key_idx = lax.broadcasted_iota(jnp.int32, (BK, BQ), 0)
    qry_idx = lax.broadcasted_iota(jnp.int32, (BK, BQ), 1)
    causal = key_idx <= qry_idx

    for i in range(n_blocks):
        slot = i % 2
        q_t = qt_ref[i]
        qaug_ref[slot, :HEAD_DIM, :] = q_t
        if moba:
            gate = jnp.dot(kmean, q_t, preferred_element_type=F32)[:8, :]
            gate = jnp.where(row8 < i, gate, -jnp.inf)
            mask = jnp.zeros((8, BQ), F32)
            for j in range(i):
                g_j = gate[j:j + 1, :]
                beats = (gate > g_j) | ((gate == g_j) & (row8 < j))
                rank = jnp.sum(beats.astype(F32), axis=0, keepdims=True)
                mask = jnp.where(row8 == j, jnp.where(rank < MOBA_TOPK, 0.0, MASK_NEG), mask)
            alibi_rows = jnp.where(row8 < 3, 1.0, 0.0)
            qaug_ref[slot, HEAD_DIM:HEAD_DIM + 16, :] = jnp.concatenate(
                [alibi_rows, mask], axis=0).astype(BF16)
        q_aug = qaug_ref[slot]
        d0, kt = i * BK, (i + 1) * BK
        if i > 0:
            s_ref[slot, :d0, :] = jnp.dot(kaug_ref[:d0, :], q_aug, preferred_element_type=F32)
        s_diag = jnp.dot(kaug_ref[d0:kt, :], q_aug, preferred_element_type=F32)
        s_ref[slot, d0:kt, :] = jnp.where(causal, s_diag, -jnp.inf)
        m = jnp.max(s_ref[slot, :kt, :], axis=0, keepdims=True)
        p_ref[slot, :kt, :] = jnp.exp(s_ref[slot, :kt, :] - m).astype(BF16)
        acc = jnp.dot(vaug_ref[:, :kt], p_ref[slot, :kt, :], preferred_element_type=F32)
        out = acc[:HEAD_DIM, :] / acc[HEAD_DIM:HEAD_DIM + 1, :]
        g = gt_ref[i].astype(F32)
        o_ref[i] = (out * (g / (1.0 + jnp.exp(-g)))).astype(BF16)


def _attention(qt, kaug, vaug, gt, *, moba):
    b, h, nb = qt.shape[:3]
    s = kaug.shape[2]
    spec_q = pl.BlockSpec((None, None, nb, HEAD_DIM, BQ), lambda i, j: (i, j, 0, 0, 0))
    return pl.pallas_call(
        functools.partial(_attn_kernel, moba=moba),
        out_shape=jax.ShapeDtypeStruct((b, h, nb, HEAD_DIM, BQ), BF16),
        grid=(b, h),
        in_specs=[spec_q,
                  pl.BlockSpec((None, None, s, K_AUG), lambda i, j: (i, j, 0, 0)),
                  pl.BlockSpec((None, None, K_AUG, s), lambda i, j: (i, j, 0, 0)),
                  spec_q],
        out_specs=spec_q,
        scratch_shapes=[pltpu.VMEM((2, K_AUG, BQ), BF16),
                        pltpu.VMEM((2, s, BQ), F32),
                        pltpu.VMEM((2, s, BQ), BF16)],
        compiler_params=pltpu.CompilerParams(
            dimension_semantics=("arbitrary", "arbitrary"), vmem_limit_bytes=VMEM_LIMIT),
        name="moba_attention" if moba else "fox_attention",
    )(qt, kaug, vaug, gt)


def _out_proj_kernel(y_ref, w_ref, x_ref, g_ref, o_ref):
    y = jnp.dot(y_ref[...], w_ref[...], preferred_element_type=F32)
    ms = jnp.mean(y * y, axis=-1, keepdims=True)
    o_ref[...] = x_ref[...] + y * lax.rsqrt(ms + RMS_EPS) * g_ref[...]


def _out_proj(y2d, w, x2d, g, *, tm=512):
    n, d = x2d.shape
    return pl.pallas_call(
        _out_proj_kernel,
        out_shape=jax.ShapeDtypeStruct((n, d), F32),
        grid=(n // tm,),
        in_specs=[pl.BlockSpec((tm, y2d.shape[1]), lambda i: (i, 0)),
                  pl.BlockSpec(w.shape, lambda i: (0, 0)),
                  pl.BlockSpec((tm, d), lambda i: (i, 0)),
                  pl.BlockSpec((1, d), lambda i: (0, 0))],
        out_specs=pl.BlockSpec((tm, d), lambda i: (i, 0)),
        compiler_params=pltpu.CompilerParams(
            dimension_semantics=("arbitrary",), vmem_limit_bytes=VMEM_LIMIT),
        name="out_proj",
    )(y2d, w, x2d, g)


def _alibi_key_features(s):
    slopes = np.asarray(2.0 ** (-8.0 * np.arange(1, N_HEADS + 1) / N_HEADS), dtype=np.float32)
    bias = slopes[:, None] * np.arange(s, dtype=np.float32)[None, :]

    def top_bits(v):
        return (v.view(np.uint32) & np.uint32(0xFFFF0000)).view(np.float32)

    hi = top_bits(bias)
    mid = top_bits(bias - hi)
    lo = bias - hi - mid
    feats = np.zeros((N_HEADS, s, HEAD_DIM), np.float32)
    feats[:, :, 0], feats[:, :, 1], feats[:, :, 2] = hi, mid, lo
    feats[:, :, 8:16] = (np.arange(s)[:, None] // MOBA_BLOCK == np.arange(8)[None, :])[None]
    return jnp.asarray(feats, dtype=BF16)


def _layer(x, g_pre, w_in, f_b, w_out, g_post):
    b, s, d = x.shape
    nb = s // BQ
    gw = GROUP_WIDTH
    scale = HEAD_DIM ** -0.5
    fo = 4 * gw
    wa, wf, wb = w_in[:, :fo], w_in[:, fo:fo + N_HEADS], w_in[:, fo + N_HEADS:]
    w = jnp.concatenate([wa[:, :gw] * scale, wa[:, gw:], wb[:, :gw] * scale, wb[:, gw:]],
                        axis=1).astype(BF16)
    wf3 = jnp.concatenate([wf, wf, wf, jnp.zeros((d, 128 - 3 * N_HEADS), F32)], axis=1).astype(BF16)
    fb3 = jnp.concatenate([f_b, f_b, f_b, jnp.zeros((128 - 3 * N_HEADS,), F32)])[None, :]

    x2d = x.reshape(b * s, d)
    proj, flog = _in_proj(x2d, g_pre[None, :], w, wf3)
    f_parts = _fox_cumsum(flog.reshape(b, s, 128), fb3)

    p6 = proj.reshape(b, nb, BQ, 8, N_HEADS, HEAD_DIM)
    p5 = proj.reshape(b, s, 8, N_HEADS, HEAD_DIM)

    def t_blocks(part):
        return p6[:, :, :, part].transpose(0, 3, 1, 4, 2)

    def rows(part):
        return p5[:, :, part].transpose(0, 2, 1, 3)

    def v_aug(part):
        vt = p5[:, :, part].transpose(0, 2, 3, 1)
        ones = jnp.ones((b, N_HEADS, 1, s), BF16)
        zeros = jnp.zeros((b, N_HEADS, K_AUG - HEAD_DIM - 1, s), BF16)
        return jnp.concatenate([vt, ones, zeros], axis=2)

    shape_k = (b, N_HEADS, s, HEAD_DIM)
    kaug_a = jnp.concatenate(
        [rows(1), jnp.broadcast_to(f_parts[:, None], shape_k)], axis=-1)
    kaug_b = jnp.concatenate(
        [rows(5), jnp.broadcast_to(_alibi_key_features(s)[None], shape_k)], axis=-1)

    yt_a = _attention(t_blocks(0), kaug_a, v_aug(2), t_blocks(3), moba=False)
    yt_b = _attention(t_blocks(4), kaug_b, v_aug(6), t_blocks(7), moba=True)

    def merge(yt):
        return yt.transpose(0, 2, 4, 1, 3).reshape(b * s, gw)

    y2d = jnp.concatenate([merge(yt_a), merge(yt_b)], axis=1)
    out = _out_proj(y2d, w_out.astype(BF16), x2d, g_post[None, :])
    return out.reshape(b, s, d)


def kernel(x, norm_pre_g, w_in, fox_forget_b, w_out, norm_post_g):
    for layer in range(norm_pre_g.shape[0]):
        x = _layer(x, norm_pre_g[layer], w_in[layer], fox_forget_b[layer], w_out[layer],
                   norm_post_g[layer])
    return x
```

```python
import functools

import jax
import jax.numpy as jnp
import numpy as np
from jax import lax
from jax.experimental import pallas as pl
from jax.experimental.pallas import tpu as pltpu

D_MODEL = 1024
HEAD_DIM = 64
N_HEADS = 8
GROUP_WIDTH = N_HEADS * HEAD_DIM
MOBA_BLOCK = 256
MOBA_TOPK = 3
RMS_EPS = 1e-6

BQ = 256
BK = 256
K_AUG = 128
V_ROWS = HEAD_DIM + 16
MASK_NEG = -1e30
PAIR = 2 * HEAD_DIM
N_PAIRS = N_HEADS // 2
F32 = jnp.float32
BF16 = jnp.bfloat16

VMEM_LIMIT = 56 * 1024 * 1024
_NT = (((1,), (1,)), ((), ()))
_TN = (((0,), (0,)), ((), ()))


def _in_proj_kernel(x_ref, g_ref, wt_ref, wk_ref, t_ref, k_ref, f_ref, *, row_chunk):
    x = x_ref[...]
    ms = jnp.mean(x * x, axis=-1, keepdims=True)
    h = (x * lax.rsqrt(ms + RMS_EPS) * g_ref[...]).astype(BF16)
    for c in range(0, wt_ref.shape[0], row_chunk):
        t_ref[c:c + row_chunk, :] = lax.dot_general(
            wt_ref[c:c + row_chunk, :], h, _NT, preferred_element_type=F32).astype(BF16)
    kf = jnp.dot(h, wk_ref[...], preferred_element_type=F32)
    n_k = k_ref.shape[1]
    k_ref[...] = kf[:, :n_k].astype(BF16)
    f_ref[...] = kf[:, n_k:]


def _in_proj(x2d, g, wt, wk):
    n, d = x2d.shape
    n_t = wt.shape[0]
    n_k = wk.shape[1] - 128
    return pl.pallas_call(
        functools.partial(_in_proj_kernel, row_chunk=512),
        out_shape=(jax.ShapeDtypeStruct((n // BQ, n_t, BQ), BF16),
                   jax.ShapeDtypeStruct((n, n_k), BF16),
                   jax.ShapeDtypeStruct((n, 128), F32)),
        grid=(n // BQ,),
        in_specs=[pl.BlockSpec((BQ, d), lambda i: (i, 0)),
                  pl.BlockSpec((1, d), lambda i: (0, 0)),
                  pl.BlockSpec((n_t, d), lambda i: (0, 0)),
                  pl.BlockSpec((d, n_k + 128), lambda i: (0, 0))],
        out_specs=(pl.BlockSpec((None, n_t, BQ), lambda i: (i, 0, 0)),
                   pl.BlockSpec((BQ, n_k), lambda i: (i, 0)),
                   pl.BlockSpec((BQ, 128), lambda i: (i, 0))),
        compiler_params=pltpu.CompilerParams(
            dimension_semantics=("arbitrary",), vmem_limit_bytes=VMEM_LIMIT),
        name="in_proj",
    )(x2d, g, wt, wk)


def _split3(v):
    hi = v.astype(BF16)
    r = v - hi.astype(F32)
    mid = r.astype(BF16)
    lo = (r - mid.astype(F32)).astype(BF16)
    return hi, mid, lo


def _fox_cumsum_kernel(f_ref, b_ref, o_ref, *, chunk):
    s = f_ref.shape[1]
    rows = lax.broadcasted_iota(jnp.int32, (chunk, chunk), 0)
    cols = lax.broadcasted_iota(jnp.int32, (chunk, chunk), 1)
    tri = (rows >= cols).astype(F32)
    lane = lax.broadcasted_iota(jnp.int32, (chunk, 128), 1) & (HEAD_DIM - 1)
    carry = jnp.zeros((1, 128), F32)
    for c in range(0, s, chunk):
        z = f_ref[0, c:c + chunk, :] + b_ref[...]
        log_f = jnp.minimum(z, 0.0) - jnp.log1p(jnp.exp(-jnp.abs(z)))
        cum = jnp.dot(tri, log_f, preferred_element_type=F32,
                      precision=lax.Precision.HIGHEST) + carry
        carry = cum[chunk - 1:chunk, :]
        hi, mid, lo = _split3(cum)
        zero = jnp.zeros_like(hi)
        o_ref[0, c:c + chunk, :] = jnp.where(
            lane < 8, hi, jnp.where(lane < 16, mid, jnp.where(lane < 24, lo, zero)))


def _fox_cumsum(flog, bias_row):
    b, s, _ = flog.shape
    return pl.pallas_call(
        functools.partial(_fox_cumsum_kernel, chunk=256),
        out_shape=jax.ShapeDtypeStruct((b, s, 128), BF16),
        grid=(b,),
        in_specs=[pl.BlockSpec((1, s, 128), lambda i: (i, 0, 0)),
                  pl.BlockSpec((1, 128), lambda i: (0, 0))],
        out_specs=pl.BlockSpec((1, s, 128), lambda i: (i, 0, 0)),
        compiler_params=pltpu.CompilerParams(dimension_semantics=("arbitrary",)),
        name="fox_cumsum",
    )(flog, bias_row)


def _attn_kernel(q_ref, k_ref, v_ref, g_ref, feat_ref, o_ref,
                 qaug_ref, kaug_ref, vaug_ref, s_ref, p_ref, *, moba):
    n_blocks = q_ref.shape[0]
    seq = k_ref.shape[0]
    pair = pl.program_id(1)
    row64 = lax.broadcasted_iota(jnp.int32, (HEAD_DIM, BQ), 0)
    row8 = lax.broadcasted_iota(jnp.int32, (8, BQ), 0)
    row16 = lax.broadcasted_iota(jnp.int32, (16, K_AUG), 0)
    lane_k = lax.broadcasted_iota(jnp.int32, (seq, K_AUG), 1)
    key_idx = lax.broadcasted_iota(jnp.int32, (BK, BQ), 0)
    qry_idx = lax.broadcasted_iota(jnp.int32, (BK, BQ), 1)
    causal = key_idx <= qry_idx
    ones_rows = jnp.where(lax.broadcasted_iota(jnp.int32, (V_ROWS - HEAD_DIM, seq), 0) == 0,
                          1.0, 0.0).astype(BF16)

    for e in range(2):
        head = 2 * pair + e
        r0, t0 = e * HEAD_DIM, (1 - e) * HEAD_DIM
        own_lanes = (lane_k >= r0) & (lane_k < r0 + HEAD_DIM)
        kaug_ref[e] = jnp.where(own_lanes, k_ref[...], feat_ref[e if moba else 0])
        vaug_ref[e, HEAD_DIM:, :] = ones_rows
        for i in range(n_blocks):
            vaug_ref[e, :HEAD_DIM, i * BQ:(i + 1) * BQ] = v_ref[i, r0:r0 + HEAD_DIM, :]
        if moba:
            tail = jnp.where(row64 < 3, 1.0, 0.0).astype(BF16)
            km = jnp.zeros((16, K_AUG), F32)
            for j in range(n_blocks):
                blk = kaug_ref[e, j * BK:(j + 1) * BK, :].astype(F32)
                km = jnp.where(row16 == j, jnp.sum(blk, axis=0, keepdims=True) * (1.0 / BK), km)
            kmean = km[:, r0:r0 + HEAD_DIM].astype(BF16)
        else:
            sel = (row64 == head) | (row64 == head + 8) | (row64 == head + 16)
            tail = jnp.where(sel, -1.0, 0.0).astype(BF16)
        qaug_ref[e, 0, t0:t0 + HEAD_DIM, :] = tail
        qaug_ref[e, 1, t0:t0 + HEAD_DIM, :] = tail

        for i in range(n_blocks):
            slot = i % 2
            q_t = q_ref[i, r0:r0 + HEAD_DIM, :]
            qaug_ref[e, slot, r0:r0 + HEAD_DIM, :] = q_t
            if moba:
                gate = jnp.dot(kmean, q_t, preferred_element_type=F32)[:8, :]
                gate = jnp.where(row8 < i, gate, -jnp.inf)
                mask = jnp.zeros((8, BQ), F32)
                for j in range(i):
                    g_j = gate[j:j + 1, :]
                    beats = (gate > g_j) | ((gate == g_j) & (row8 < j))
                    rank = jnp.sum(beats.astype(F32), axis=0, keepdims=True)
                    mask = jnp.where(row8 == j, jnp.where(rank < MOBA_TOPK, 0.0, MASK_NEG), mask)
                alibi_rows = jnp.where(row8 < 3, 1.0, 0.0)
                qaug_ref[e, slot, t0:t0 + 16, :] = jnp.concatenate(
                    [alibi_rows, mask], axis=0).astype(BF16)
            q_aug = qaug_ref[e, slot]
            d0, kt = i * BK, (i + 1) * BK
            if i > 0:
                s_ref[slot, :d0, :] = jnp.dot(kaug_ref[e, :d0, :], q_aug,
                                              preferred_element_type=F32)
            s_diag = jnp.dot(kaug_ref[e, d0:kt, :], q_aug, preferred_element_type=F32)
            s_ref[slot, d0:kt, :] = jnp.where(causal, s_diag, -jnp.inf)
            m = jnp.max(s_ref[slot, :kt, :], axis=0, keepdims=True)
            p_ref[slot, :kt, :] = jnp.exp(s_ref[slot, :kt, :] - m).astype(BF16)
            acc = jnp.dot(vaug_ref[e, :, :kt], p_ref[slot, :kt, :], preferred_element_type=F32)
            out = acc[:HEAD_DIM, :] / acc[HEAD_DIM:HEAD_DIM + 1, :]
            g = g_ref[i, r0:r0 + HEAD_DIM, :].astype(F32)
            o_ref[i, r0:r0 + HEAD_DIM, :] = (out * (g / (1.0 + jnp.exp(-g)))).astype(BF16)


def _attention(t_all, k_all, feat, *, group, moba, batch):
    nb = t_all.shape[0] // batch
    seq = nb * BQ
    t4 = t_all.reshape(batch, nb, t_all.shape[1], BQ)
    k3 = k_all.reshape(batch, seq, k_all.shape[1])
    blocks_per_part = 2 * N_PAIRS

    def part_spec(part):
        off = part * blocks_per_part + group * N_PAIRS
        return pl.BlockSpec((None, nb, PAIR, BQ), lambda b, j: (b, 0, off + j, 0))

    if moba:
        feat_spec = pl.BlockSpec((2, seq, K_AUG), lambda b, j: (j, 0, 0))
    else:
        feat_spec = pl.BlockSpec((1, seq, K_AUG), lambda b, j: (b, 0, 0))
    return pl.pallas_call(
        functools.partial(_attn_kernel, moba=moba),
        out_shape=jax.ShapeDtypeStruct((batch, nb, GROUP_WIDTH, BQ), BF16),
        grid=(batch, N_PAIRS),
        in_specs=[part_spec(0),
                  pl.BlockSpec((None, seq, PAIR), lambda b, j: (b, 0, group * N_PAIRS + j)),
                  part_spec(1), part_spec(2), feat_spec],
        out_specs=pl.BlockSpec((None, nb, PAIR, BQ), lambda b, j: (b, 0, j, 0)),
        scratch_shapes=[pltpu.VMEM((2, 2, K_AUG, BQ), BF16),
                        pltpu.VMEM((2, seq, K_AUG), BF16),
                        pltpu.VMEM((2, V_ROWS, seq), BF16),
                        pltpu.VMEM((2, seq, BQ), F32),
                        pltpu.VMEM((2, seq, BQ), BF16)],
        compiler_params=pltpu.CompilerParams(
            dimension_semantics=("arbitrary", "arbitrary"), vmem_limit_bytes=VMEM_LIMIT),
        name="moba_attention" if moba else "fox_attention",
    )(t4, k3, t4, t4, feat)


def _out_proj_kernel(ya_ref, yb_ref, wa_ref, wb_ref, x_ref, g_ref, o_ref):
    y = (lax.dot_general(ya_ref[...], wa_ref[...], _TN, preferred_element_type=F32)
         + lax.dot_general(yb_ref[...], wb_ref[...], _TN, preferred_element_type=F32))
    ms = jnp.mean(y * y, axis=-1, keepdims=True)
    o_ref[...] = x_ref[...] + y * lax.rsqrt(ms + RMS_EPS) * g_ref[...]


def _out_proj(yt_a, yt_b, w_a, w_b, x2d, g):
    n, d = x2d.shape
    gw = yt_a.shape[2]
    nb = yt_a.shape[1]
    y_spec = pl.BlockSpec((None, None, gw, BQ), lambda i: (i // nb, i % nb, 0, 0))
    return pl.pallas_call(
        _out_proj_kernel,
        out_shape=jax.ShapeDtypeStruct((n, d), F32),
        grid=(n // BQ,),
        in_specs=[y_spec, y_spec,
                  pl.BlockSpec((gw, d), lambda i: (0, 0)),
                  pl.BlockSpec((gw, d), lambda i: (0, 0)),
                  pl.BlockSpec((BQ, d), lambda i: (i, 0)),
                  pl.BlockSpec((1, d), lambda i: (0, 0))],
        out_specs=pl.BlockSpec((BQ, d), lambda i: (i, 0)),
        compiler_params=pltpu.CompilerParams(
            dimension_semantics=("arbitrary",), vmem_limit_bytes=VMEM_LIMIT),
        name="out_proj",
    )(yt_a, yt_b, w_a, w_b, x2d, g)


def _alibi_key_features(s):
    slopes = np.asarray(2.0 ** (-8.0 * np.arange(1, N_HEADS + 1) / N_HEADS), dtype=np.float32)
    bias = slopes[:, None] * np.arange(s, dtype=np.float32)[None, :]

    def top_bits(v):
        return (v.view(np.uint32) & np.uint32(0xFFFF0000)).view(np.float32)

    hi = top_bits(bias)
    mid = top_bits(bias - hi)
    lo = bias - hi - mid
    feats = np.zeros((N_HEADS, s, HEAD_DIM), np.float32)
    feats[:, :, 0], feats[:, :, 1], feats[:, :, 2] = hi, mid, lo
    feats[:, :, 8:16] = (np.arange(s)[:, None] // MOBA_BLOCK == np.arange(8)[None, :])[None]
    return jnp.asarray(np.concatenate([feats, feats], axis=-1), dtype=BF16)


def _layer(x, g_pre, w_in, f_b, w_out, g_post):
    b, s, d = x.shape
    gw = GROUP_WIDTH
    scale = HEAD_DIM ** -0.5
    fo = 4 * gw
    wa, wf, wb = w_in[:, :fo], w_in[:, fo:fo + N_HEADS], w_in[:, fo + N_HEADS:]

    def cols(w, part):
        return w[:, part * gw:(part + 1) * gw]

    wt = jnp.concatenate([cols(wa, 0) * scale, cols(wb, 0) * scale, cols(wa, 2), cols(wb, 2),
                          cols(wa, 3), cols(wb, 3)], axis=1).T.astype(BF16)
    pad = jnp.zeros((d, HEAD_DIM - 3 * N_HEADS), F32)
    wk = jnp.concatenate([cols(wa, 1), cols(wb, 1), wf, wf, wf, pad, wf, wf, wf, pad],
                         axis=1).astype(BF16)
    zpad = jnp.zeros((HEAD_DIM - 3 * N_HEADS,), F32)
    fb = jnp.concatenate([f_b, f_b, f_b, zpad, f_b, f_b, f_b, zpad])[None, :]

    x2d = x.reshape(b * s, d)
    t_all, k_all, flog = _in_proj(x2d, g_pre[None, :], wt, wk)
    f_parts = _fox_cumsum(flog.reshape(b, s, 128), fb)

    yt_a = _attention(t_all, k_all, f_parts, group=0, moba=False, batch=b)
    yt_b = _attention(t_all, k_all, _alibi_key_features(s), group=1, moba=True, batch=b)

    w_o = w_out.astype(BF16)
    out = _out_proj(yt_a, yt_b, w_o[:gw], w_o[gw:], x2d, g_post[None, :])
    return out.reshape(b, s, d)


def kernel(x, norm_pre_g, w_in, fox_forget_b, w_out, norm_post_g):
    for layer in range(norm_pre_g.shape[0]):
        x = _layer(x, norm_pre_g[layer], w_in[layer], fox_forget_b[layer], w_out[layer],
                   norm_post_g[layer])
    return x
```

```python
import functools

import jax
import jax.numpy as jnp
import numpy as np
from jax import lax
from jax.experimental import pallas as pl
from jax.experimental.pallas import tpu as pltpu

D_MODEL = 1024
HEAD_DIM = 64
N_HEADS = 8
GROUP_WIDTH = N_HEADS * HEAD_DIM
MOBA_BLOCK = 256
MOBA_TOPK = 3
RMS_EPS = 1e-6

BQ = 256
BK = 256
K_AUG = 128
V_ROWS = HEAD_DIM + 16
LOG2E = 1.4426950408889634
MASK_NEG = -1e30
PAIR = 2 * HEAD_DIM
HPS = 4
F32 = jnp.float32
BF16 = jnp.bfloat16

VMEM_LIMIT = 56 * 1024 * 1024
_NT = (((1,), (1,)), ((), ()))
_TN = (((0,), (0,)), ((), ()))


def _in_proj_kernel(x_ref, g_ref, wt_ref, wk_ref, t_ref, k_ref, f_ref, *, row_chunk):
    x = x_ref[...]
    ms = jnp.mean(x * x, axis=-1, keepdims=True)
    h = (x * lax.rsqrt(ms + RMS_EPS) * g_ref[...]).astype(BF16)
    for c in range(0, wt_ref.shape[0], row_chunk):
        t_ref[c:c + row_chunk, :] = lax.dot_general(
            wt_ref[c:c + row_chunk, :], h, _NT, preferred_element_type=F32).astype(BF16)
    kf = jnp.dot(h, wk_ref[...], preferred_element_type=F32)
    n_k = k_ref.shape[1]
    k_ref[...] = kf[:, :n_k].astype(BF16)
    f_ref[...] = kf[:, n_k:]


def _in_proj(x2d, g, wt, wk):
    n, d = x2d.shape
    n_t = wt.shape[0]
    n_k = wk.shape[1] - 128
    return pl.pallas_call(
        functools.partial(_in_proj_kernel, row_chunk=512),
        out_shape=(jax.ShapeDtypeStruct((n // BQ, n_t, BQ), BF16),
                   jax.ShapeDtypeStruct((n, n_k), BF16),
                   jax.ShapeDtypeStruct((n, 128), F32)),
        grid=(n // BQ,),
        in_specs=[pl.BlockSpec((BQ, d), lambda i: (i, 0)),
                  pl.BlockSpec((1, d), lambda i: (0, 0)),
                  pl.BlockSpec((n_t, d), lambda i: (0, 0)),
                  pl.BlockSpec((d, n_k + 128), lambda i: (0, 0))],
        out_specs=(pl.BlockSpec((None, n_t, BQ), lambda i: (i, 0, 0)),
                   pl.BlockSpec((BQ, n_k), lambda i: (i, 0)),
                   pl.BlockSpec((BQ, 128), lambda i: (i, 0))),
        compiler_params=pltpu.CompilerParams(
            dimension_semantics=("arbitrary",), vmem_limit_bytes=VMEM_LIMIT),
        name="in_proj",
    )(x2d, g, wt, wk)


def _split3(v):
    hi = v.astype(BF16)
    r = v - hi.astype(F32)
    mid = r.astype(BF16)
    lo = (r - mid.astype(F32)).astype(BF16)
    return hi, mid, lo


def _fox_cumsum_kernel(f_ref, b_ref, o_ref, *, chunk):
    s = f_ref.shape[1]
    rows = lax.broadcasted_iota(jnp.int32, (chunk, chunk), 0)
    cols = lax.broadcasted_iota(jnp.int32, (chunk, chunk), 1)
    tri = (rows >= cols).astype(F32)
    lane = lax.broadcasted_iota(jnp.int32, (chunk, 128), 1) & (HEAD_DIM - 1)
    carry = jnp.zeros((1, 128), F32)
    for c in range(0, s, chunk):
        z = f_ref[0, c:c + chunk, :] + b_ref[...]
        log_f = jnp.minimum(z, 0.0) - jnp.log1p(jnp.exp(-jnp.abs(z)))
        cum = jnp.dot(tri, log_f, preferred_element_type=F32,
                      precision=lax.Precision.HIGHEST) + carry
        carry = cum[chunk - 1:chunk, :]
        hi, mid, lo = _split3(cum * LOG2E)
        zero = jnp.zeros_like(hi)
        o_ref[0, c:c + chunk, :] = jnp.where(
            lane < 8, hi, jnp.where(lane < 16, mid, jnp.where(lane < 24, lo, zero)))


def _fox_cumsum(flog, bias_row):
    b, s, _ = flog.shape
    return pl.pallas_call(
        functools.partial(_fox_cumsum_kernel, chunk=256),
        out_shape=jax.ShapeDtypeStruct((b, s, 128), BF16),
        grid=(b,),
        in_specs=[pl.BlockSpec((1, s, 128), lambda i: (i, 0, 0)),
                  pl.BlockSpec((1, 128), lambda i: (0, 0))],
        out_specs=pl.BlockSpec((1, s, 128), lambda i: (i, 0, 0)),
        compiler_params=pltpu.CompilerParams(dimension_semantics=("arbitrary",)),
        name="fox_cumsum",
    )(flog, bias_row)


def _attn_kernel(q_ref, k_ref, v_ref, g_ref, feat_ref, o_ref,
                 qaug_ref, kaug_ref, vaug_ref, s_ref, p_ref, *, moba, n_heads):
    n_blocks = q_ref.shape[0]
    seq = k_ref.shape[0]
    step = pl.program_id(1)
    row64 = lax.broadcasted_iota(jnp.int32, (HEAD_DIM, BQ), 0)
    row8 = lax.broadcasted_iota(jnp.int32, (8, BQ), 0)
    row16 = lax.broadcasted_iota(jnp.int32, (16, K_AUG), 0)
    lane_k = lax.broadcasted_iota(jnp.int32, (seq, K_AUG), 1)
    key_idx = lax.broadcasted_iota(jnp.int32, (BK, BQ), 0)
    qry_idx = lax.broadcasted_iota(jnp.int32, (BK, BQ), 1)
    causal = key_idx <= qry_idx
    ones_rows = jnp.where(lax.broadcasted_iota(jnp.int32, (V_ROWS - HEAD_DIM, seq), 0) == 0,
                          1.0, 0.0).astype(BF16)

    kmeans = []
    for e in range(n_heads):
        head = n_heads * step + e
        h0 = e * HEAD_DIM
        r0, t0 = (e % 2) * HEAD_DIM, (1 - e % 2) * HEAD_DIM
        own_lanes = (lane_k >= r0) & (lane_k < r0 + HEAD_DIM)
        k_pair = k_ref[:, (e // 2) * PAIR:(e // 2 + 1) * PAIR]
        kaug_ref[e] = jnp.where(own_lanes, k_pair, feat_ref[e if moba else 0])
        vaug_ref[e, HEAD_DIM:, :] = ones_rows
        for i in range(n_blocks):
            vaug_ref[e, :HEAD_DIM, i * BQ:(i + 1) * BQ] = v_ref[i, h0:h0 + HEAD_DIM, :]
        if moba:
            tail = jnp.where(row64 < 3, 1.0, 0.0).astype(BF16)
            km = jnp.zeros((16, K_AUG), F32)
            for j in range(n_blocks):
                blk = kaug_ref[e, j * BK:(j + 1) * BK, :].astype(F32)
                km = jnp.where(row16 == j, jnp.sum(blk, axis=0, keepdims=True) * (1.0 / BK), km)
            kmeans.append(km[:, r0:r0 + HEAD_DIM].astype(BF16))
        else:
            sel = (row64 == head) | (row64 == head + 8) | (row64 == head + 16)
            tail = jnp.where(sel, -1.0, 0.0).astype(BF16)
        qaug_ref[e, t0:t0 + HEAD_DIM, :] = tail

    for i in range(n_blocks):
        d0, kt = i * BK, (i + 1) * BK
        for e in range(n_heads):
            h0 = e * HEAD_DIM
            r0, t0 = (e % 2) * HEAD_DIM, (1 - e % 2) * HEAD_DIM
            q_t = q_ref[i, h0:h0 + HEAD_DIM, :]
            qaug_ref[e, r0:r0 + HEAD_DIM, :] = q_t
            if moba:
                gate = jnp.dot(kmeans[e], q_t, preferred_element_type=F32)[:8, :]
                gate = jnp.where(row8 < i, gate, -jnp.inf)
                mask = jnp.zeros((8, BQ), F32)
                for j in range(i):
                    g_j = gate[j:j + 1, :]
                    beats = (gate > g_j) | ((gate == g_j) & (row8 < j))
                    rank = jnp.sum(beats.astype(F32), axis=0, keepdims=True)
                    mask = jnp.where(row8 == j, jnp.where(rank < MOBA_TOPK, 0.0, MASK_NEG), mask)
                alibi_rows = jnp.where(row8 < 3, 1.0, 0.0)
                qaug_ref[e, t0:t0 + 16, :] = jnp.concatenate(
                    [alibi_rows, mask], axis=0).astype(BF16)
        maxes = []
        for e in range(n_heads):
            q_aug = qaug_ref[e]
            s_diag = jnp.dot(kaug_ref[e, d0:kt, :], q_aug, preferred_element_type=F32)
            s_diag = jnp.where(causal, s_diag, -jnp.inf)
            s_ref[e, d0:kt, :] = s_diag
            mx = jnp.max(s_diag.reshape(BK // 8, 8, BQ), axis=0)
            if i > 0:
                s_past = jnp.dot(kaug_ref[e, :d0, :], q_aug, preferred_element_type=F32)
                s_ref[e, :d0, :] = s_past
                mx = jnp.maximum(mx, jnp.max(s_past.reshape(d0 // 8, 8, BQ), axis=0))
            maxes.append(jnp.max(mx, axis=0, keepdims=True))
        for e in range(n_heads):
            p_ref[e, :kt, :] = jnp.exp2(s_ref[e, :kt, :] - maxes[e]).astype(BF16)
        for e in range(n_heads):
            h0 = e * HEAD_DIM
            acc = jnp.dot(vaug_ref[e, :, :kt], p_ref[e, :kt, :], preferred_element_type=F32)
            out = acc[:HEAD_DIM, :] / acc[HEAD_DIM:HEAD_DIM + 1, :]
            g = g_ref[i, h0:h0 + HEAD_DIM, :].astype(F32)
            o_ref[i, h0:h0 + HEAD_DIM, :] = (out * (g / (1.0 + jnp.exp(-g)))).astype(BF16)


def _attention(t_all, k_all, feat, *, group, moba, batch, n_heads=HPS):
    nb = t_all.shape[0] // batch
    seq = nb * BQ
    t4 = t_all.reshape(batch, nb, t_all.shape[1], BQ)
    k3 = k_all.reshape(batch, seq, k_all.shape[1])
    n_steps = N_HEADS // n_heads
    width = n_heads * HEAD_DIM

    def part_spec(part):
        off = (2 * part + group) * n_steps
        return pl.BlockSpec((None, nb, width, BQ), lambda b, j: (b, 0, off + j, 0))

    if moba:
        feat_spec = pl.BlockSpec((n_heads, seq, K_AUG), lambda b, j: (j, 0, 0))
    else:
        feat_spec = pl.BlockSpec((1, seq, K_AUG), lambda b, j: (b, 0, 0))
    return pl.pallas_call(
        functools.partial(_attn_kernel, moba=moba, n_heads=n_heads),
        out_shape=jax.ShapeDtypeStruct((batch, nb, GROUP_WIDTH, BQ), BF16),
        grid=(batch, n_steps),
        in_specs=[part_spec(0),
                  pl.BlockSpec((None, seq, width), lambda b, j: (b, 0, group * n_steps + j)),
                  part_spec(1), part_spec(2), feat_spec],
        out_specs=pl.BlockSpec((None, nb, width, BQ), lambda b, j: (b, 0, j, 0)),
        scratch_shapes=[pltpu.VMEM((n_heads, K_AUG, BQ), BF16),
                        pltpu.VMEM((n_heads, seq, K_AUG), BF16),
                        pltpu.VMEM((n_heads, V_ROWS, seq), BF16),
                        pltpu.VMEM((n_heads, seq, BQ), F32),
                        pltpu.VMEM((n_heads, seq, BQ), BF16)],
        compiler_params=pltpu.CompilerParams(
            dimension_semantics=("arbitrary", "arbitrary"), vmem_limit_bytes=VMEM_LIMIT),
        name="moba_attention" if moba else "fox_attention",
    )(t4, k3, t4, t4, feat)


def _out_proj_kernel(ya_ref, yb_ref, wa_ref, wb_ref, x_ref, g_ref, o_ref):
    y = (lax.dot_general(ya_ref[...], wa_ref[...], _TN, preferred_element_type=F32)
         + lax.dot_general(yb_ref[...], wb_ref[...], _TN, preferred_element_type=F32))
    ms = jnp.mean(y * y, axis=-1, keepdims=True)
    o_ref[...] = x_ref[...] + y * lax.rsqrt(ms + RMS_EPS) * g_ref[...]


def _out_proj(yt_a, yt_b, w_a, w_b, x2d, g):
    n, d = x2d.shape
    gw = yt_a.shape[2]
    nb = yt_a.shape[1]
    y_spec = pl.BlockSpec((None, None, gw, BQ), lambda i: (i // nb, i % nb, 0, 0))
    return pl.pallas_call(
        _out_proj_kernel,
        out_shape=jax.ShapeDtypeStruct((n, d), F32),
        grid=(n // BQ,),
        in_specs=[y_spec, y_spec,
                  pl.BlockSpec((gw, d), lambda i: (0, 0)),
                  pl.BlockSpec((gw, d), lambda i: (0, 0)),
                  pl.BlockSpec((BQ, d), lambda i: (i, 0)),
                  pl.BlockSpec((1, d), lambda i: (0, 0))],
        out_specs=pl.BlockSpec((BQ, d), lambda i: (i, 0)),
        compiler_params=pltpu.CompilerParams(
            dimension_semantics=("arbitrary",), vmem_limit_bytes=VMEM_LIMIT),
        name="out_proj",
    )(yt_a, yt_b, w_a, w_b, x2d, g)


def _alibi_key_features(s):
    slopes = np.asarray(2.0 ** (-8.0 * np.arange(1, N_HEADS + 1) / N_HEADS), dtype=np.float32)
    bias = (slopes[:, None].astype(np.float64) * np.arange(s)[None, :] * LOG2E).astype(np.float32)

    def top_bits(v):
        return (v.view(np.uint32) & np.uint32(0xFFFF0000)).view(np.float32)

    hi = top_bits(bias)
    mid = top_bits(bias - hi)
    lo = bias - hi - mid
    feats = np.zeros((N_HEADS, s, HEAD_DIM), np.float32)
    feats[:, :, 0], feats[:, :, 1], feats[:, :, 2] = hi, mid, lo
    feats[:, :, 8:16] = (np.arange(s)[:, None] // MOBA_BLOCK == np.arange(8)[None, :])[None]
    return jnp.asarray(np.concatenate([feats, feats], axis=-1), dtype=BF16)


def _layer(x, g_pre, w_in, f_b, w_out, g_post):
    b, s, d = x.shape
    gw = GROUP_WIDTH
    scale = HEAD_DIM ** -0.5 * LOG2E
    fo = 4 * gw
    wa, wf, wb = w_in[:, :fo], w_in[:, fo:fo + N_HEADS], w_in[:, fo + N_HEADS:]

    def cols(w, part):
        return w[:, part * gw:(part + 1) * gw]

    wt = jnp.concatenate([cols(wa, 0) * scale, cols(wb, 0) * scale, cols(wa, 2), cols(wb, 2),
                          cols(wa, 3), cols(wb, 3)], axis=1).T.astype(BF16)
    pad = jnp.zeros((d, HEAD_DIM - 3 * N_HEADS), F32)
    wk = jnp.concatenate([cols(wa, 1), cols(wb, 1), wf, wf, wf, pad, wf, wf, wf, pad],
                         axis=1).astype(BF16)
    zpad = jnp.zeros((HEAD_DIM - 3 * N_HEADS,), F32)
    fb = jnp.concatenate([f_b, f_b, f_b, zpad, f_b, f_b, f_b, zpad])[None, :]

    x2d = x.reshape(b * s, d)
    t_all, k_all, flog = _in_proj(x2d, g_pre[None, :], wt, wk)
    f_parts = _fox_cumsum(flog.reshape(b, s, 128), fb)

    yt_a = _attention(t_all, k_all, f_parts, group=0, moba=False, batch=b)
    yt_b = _attention(t_all, k_all, _alibi_key_features(s), group=1, moba=True, batch=b)

    w_o = w_out.astype(BF16)
    out = _out_proj(yt_a, yt_b, w_o[:gw], w_o[gw:], x2d, g_post[None, :])
    return out.reshape(b, s, d)


def kernel(x, norm_pre_g, w_in, fox_forget_b, w_out, norm_post_g):
    for layer in range(norm_pre_g.shape[0]):
        x = _layer(x, norm_pre_g[layer], w_in[layer], fox_forget_b[layer], w_out[layer],
                   norm_post_g[layer])
    return x
```

```python
import functools

import jax
import jax.numpy as jnp
import numpy as np
from jax import lax
from jax.experimental import pallas as pl
from jax.experimental.pallas import tpu as pltpu

D_MODEL = 1024
HEAD_DIM = 64
N_HEADS = 8
GROUP_WIDTH = N_HEADS * HEAD_DIM
MOBA_BLOCK = 256
MOBA_TOPK = 3
RMS_EPS = 1e-6

BQ = 256
BK = 256
K_AUG = 128
V_ROWS = HEAD_DIM + 16
LOG2E = 1.4426950408889634
MASK_NEG = -1e30
PAIR = 2 * HEAD_DIM
HPS = 4
IN_BLOCKS = 2
OUT_BLOCKS = 4
F32 = jnp.float32
BF16 = jnp.bfloat16

VMEM_LIMIT = 56 * 1024 * 1024
_NT = (((1,), (1,)), ((), ()))
_TN = (((0,), (0,)), ((), ()))


def _in_proj_kernel(x_ref, g_ref, wt_ref, wk_ref, t_ref, k_ref, f_ref, *, row_chunk):
    n_k = k_ref.shape[1]
    for r in range(t_ref.shape[0]):
        x = x_ref[r * BQ:(r + 1) * BQ, :]
        ms = jnp.mean(x * x, axis=-1, keepdims=True)
        h = (x * lax.rsqrt(ms + RMS_EPS) * g_ref[...]).astype(BF16)
        for c in range(0, wt_ref.shape[0], row_chunk):
            t_ref[r, c:c + row_chunk, :] = lax.dot_general(
                wt_ref[c:c + row_chunk, :], h, _NT, preferred_element_type=F32).astype(BF16)
        kf = jnp.dot(h, wk_ref[...], preferred_element_type=F32)
        k_ref[r * BQ:(r + 1) * BQ, :] = kf[:, :n_k].astype(BF16)
        f_ref[r * BQ:(r + 1) * BQ, :] = kf[:, n_k:]


def _in_proj(x2d, g, wt, wk, *, blocks_per_step=IN_BLOCKS):
    n, d = x2d.shape
    n_t = wt.shape[0]
    n_k = wk.shape[1] - 128
    tm = blocks_per_step * BQ
    return pl.pallas_call(
        functools.partial(_in_proj_kernel, row_chunk=512),
        out_shape=(jax.ShapeDtypeStruct((n // BQ, n_t, BQ), BF16),
                   jax.ShapeDtypeStruct((n, n_k), BF16),
                   jax.ShapeDtypeStruct((n, 128), F32)),
        grid=(n // tm,),
        in_specs=[pl.BlockSpec((tm, d), lambda i: (i, 0)),
                  pl.BlockSpec((1, d), lambda i: (0, 0)),
                  pl.BlockSpec((n_t, d), lambda i: (0, 0)),
                  pl.BlockSpec((d, n_k + 128), lambda i: (0, 0))],
        out_specs=(pl.BlockSpec((blocks_per_step, n_t, BQ), lambda i: (i, 0, 0)),
                   pl.BlockSpec((tm, n_k), lambda i: (i, 0)),
                   pl.BlockSpec((tm, 128), lambda i: (i, 0))),
        compiler_params=pltpu.CompilerParams(
            dimension_semantics=("arbitrary",), vmem_limit_bytes=VMEM_LIMIT),
        name="in_proj",
    )(x2d, g, wt, wk)


def _split3(v):
    hi = v.astype(BF16)
    r = v - hi.astype(F32)
    mid = r.astype(BF16)
    lo = (r - mid.astype(F32)).astype(BF16)
    return hi, mid, lo


def _fox_cumsum_kernel(f_ref, b_ref, o_ref, *, chunk):
    s = f_ref.shape[1]
    rows = lax.broadcasted_iota(jnp.int32, (chunk, chunk), 0)
    cols = lax.broadcasted_iota(jnp.int32, (chunk, chunk), 1)
    tri = (rows >= cols).astype(F32)
    lane = lax.broadcasted_iota(jnp.int32, (chunk, 128), 1) & (HEAD_DIM - 1)
    carry = jnp.zeros((1, 128), F32)
    for c in range(0, s, chunk):
        z = f_ref[0, c:c + chunk, :] + b_ref[...]
        log_f = jnp.minimum(z, 0.0) - jnp.log1p(jnp.exp(-jnp.abs(z)))
        cum = jnp.dot(tri, log_f, preferred_element_type=F32,
                      precision=lax.Precision.HIGHEST) + carry
        carry = cum[chunk - 1:chunk, :]
        hi, mid, lo = _split3(cum * LOG2E)
        zero = jnp.zeros_like(hi)
        o_ref[0, c:c + chunk, :] = jnp.where(
            lane < 8, hi, jnp.where(lane < 16, mid, jnp.where(lane < 24, lo, zero)))


def _fox_cumsum(flog, bias_row):
    b, s, _ = flog.shape
    return pl.pallas_call(
        functools.partial(_fox_cumsum_kernel, chunk=256),
        out_shape=jax.ShapeDtypeStruct((b, s, 128), BF16),
        grid=(b,),
        in_specs=[pl.BlockSpec((1, s, 128), lambda i: (i, 0, 0)),
                  pl.BlockSpec((1, 128), lambda i: (0, 0))],
        out_specs=pl.BlockSpec((1, s, 128), lambda i: (i, 0, 0)),
        compiler_params=pltpu.CompilerParams(dimension_semantics=("arbitrary",)),
        name="fox_cumsum",
    )(flog, bias_row)


def _attn_kernel(q_ref, k_ref, v_ref, g_ref, feat_ref, o_ref,
                 qaug_ref, kaug_ref, vaug_ref, s_ref, p_ref, sel_ref, *, moba, n_heads):
    n_blocks = q_ref.shape[0]
    seq = k_ref.shape[0]
    step = pl.program_id(1)
    row64 = lax.broadcasted_iota(jnp.int32, (HEAD_DIM, BQ), 0)
    row8 = lax.broadcasted_iota(jnp.int32, (8, BQ), 0)
    row16 = lax.broadcasted_iota(jnp.int32, (16, K_AUG), 0)
    lane_k = lax.broadcasted_iota(jnp.int32, (seq, K_AUG), 1)
    key_idx = lax.broadcasted_iota(jnp.int32, (BK, BQ), 0)
    qry_idx = lax.broadcasted_iota(jnp.int32, (BK, BQ), 1)
    causal = key_idx <= qry_idx
    ones_rows = jnp.where(lax.broadcasted_iota(jnp.int32, (V_ROWS - HEAD_DIM, seq), 0) == 0,
                          1.0, 0.0).astype(BF16)

    for e in range(n_heads):
        head = n_heads * step + e
        h0 = e * HEAD_DIM
        r0, t0 = (e % 2) * HEAD_DIM, (1 - e % 2) * HEAD_DIM
        own_lanes = (lane_k >= r0) & (lane_k < r0 + HEAD_DIM)
        k_pair = k_ref[:, (e // 2) * PAIR:(e // 2 + 1) * PAIR]
        kaug_ref[e] = jnp.where(own_lanes, k_pair, feat_ref[e if moba else 0])
        vaug_ref[e, HEAD_DIM:, :] = ones_rows
        for i in range(n_blocks):
            vaug_ref[e, :HEAD_DIM, i * BQ:(i + 1) * BQ] = v_ref[i, h0:h0 + HEAD_DIM, :]
        if moba:
            tail = jnp.where(row64 < 3, 1.0, 0.0).astype(BF16)
            km = jnp.zeros((16, K_AUG), F32)
            for j in range(n_blocks):
                blk = kaug_ref[e, j * BK:(j + 1) * BK, :].astype(F32)
                km = jnp.where(row16 == j, jnp.sum(blk, axis=0, keepdims=True) * (1.0 / BK), km)
            kmean = km[:, r0:r0 + HEAD_DIM].astype(BF16)
            alibi_rows = jnp.where(row8 < 3, 1.0, 0.0)
            for i in range(n_blocks):
                gate = jnp.dot(kmean, q_ref[i, h0:h0 + HEAD_DIM, :],
                               preferred_element_type=F32)[:8, :]
                gate = jnp.where(row8 < i, gate, -jnp.inf)
                mask = jnp.zeros((8, BQ), F32)
                for j in range(i):
                    g_j = gate[j:j + 1, :]
                    beats = (gate > g_j) | ((gate == g_j) & (row8 < j))
                    rank = jnp.sum(beats.astype(F32), axis=0, keepdims=True)
                    mask = jnp.where(row8 == j, jnp.where(rank < MOBA_TOPK, 0.0, MASK_NEG), mask)
                sel_ref[e, i] = jnp.concatenate([alibi_rows, mask], axis=0).astype(BF16)
        else:
            sel = (row64 == head) | (row64 == head + 8) | (row64 == head + 16)
            tail = jnp.where(sel, -1.0, 0.0).astype(BF16)
        qaug_ref[e, t0:t0 + HEAD_DIM, :] = tail

    jobs = [(i, e) for i in range(n_blocks) for e in range(n_heads)]
    job_max = {}

    def logits_ops(job):
        i, e = job
        h0 = e * HEAD_DIM
        r0, t0 = (e % 2) * HEAD_DIM, (1 - e % 2) * HEAD_DIM
        state = {}

        def setup():
            qaug_ref[e, r0:r0 + HEAD_DIM, :] = q_ref[i, h0:h0 + HEAD_DIM, :]
            if moba:
                qaug_ref[e, t0:t0 + 16, :] = sel_ref[e, i]
            state["q_aug"] = qaug_ref[e]

        def tile(t):
            def run():
                if t == 0:
                    setup()
                s_t = jnp.dot(kaug_ref[e, t * BK:(t + 1) * BK, :], state["q_aug"],
                              preferred_element_type=F32)
                if t == i:
                    s_t = jnp.where(causal, s_t, -jnp.inf)
                s_ref[e, t * BK:(t + 1) * BK, :] = s_t
                mx = jnp.max(s_t.reshape(BK // 8, 8, BQ), axis=0)
                state["mx"] = mx if t == 0 else jnp.maximum(state["mx"], mx)
                if t == i:
                    job_max[job] = jnp.max(state["mx"], axis=0, keepdims=True)
            return run
        return [tile(t) for t in range(i + 1)]

    def probs_ops(job):
        i, e = job

        def tile(t):
            def run():
                rows = slice(t * BK, (t + 1) * BK)
                p_ref[e, rows, :] = jnp.exp2(s_ref[e, rows, :] - job_max[job]).astype(BF16)
            return run
        return [tile(t) for t in range(i + 1)]

    def output_ops(job):
        i, e = job
        h0 = e * HEAD_DIM
        state = {}

        def tile(t):
            def run():
                cols = slice(t * BK, (t + 1) * BK)
                part = jnp.dot(vaug_ref[e, :, cols], p_ref[e, cols, :],
                               preferred_element_type=F32)
                state["acc"] = part if t == 0 else state["acc"] + part
                if t == i:
                    acc = state["acc"]
                    out = acc[:HEAD_DIM, :] / acc[HEAD_DIM:HEAD_DIM + 1, :]
                    g = g_ref[i, h0:h0 + HEAD_DIM, :].astype(F32)
                    o_ref[i, h0:h0 + HEAD_DIM, :] = (out * (g / (1.0 + jnp.exp(-g)))).astype(BF16)
            return run
        return [tile(t) for t in range(i + 1)]

    n_jobs = len(jobs)
    for stage in range(-1, n_jobs + 1):
        streams = []
        if 0 <= stage + 1 < n_jobs:
            streams.append(logits_ops(jobs[stage + 1]))
        if 0 <= stage < n_jobs:
            streams.append(probs_ops(jobs[stage]))
        if 0 <= stage - 1 < n_jobs:
            streams.append(output_ops(jobs[stage - 1]))
        for k in range(max(len(ops) for ops in streams)):
            for ops in streams:
                if k < len(ops):
                    ops[k]()


def _attention(t_all, k_all, feat, *, group, moba, batch, n_heads=HPS):
    nb = t_all.shape[0] // batch
    seq = nb * BQ
    t4 = t_all.reshape(batch, nb, t_all.shape[1], BQ)
    k3 = k_all.reshape(batch, seq, k_all.shape[1])
    n_steps = N_HEADS // n_heads
    width = n_heads * HEAD_DIM

    def part_spec(part):
        off = (2 * part + group) * n_steps
        return pl.BlockSpec((None, nb, width, BQ), lambda b, j: (b, 0, off + j, 0))

    if moba:
        feat_spec = pl.BlockSpec((n_heads, seq, K_AUG), lambda b, j: (j, 0, 0))
    else:
        feat_spec = pl.BlockSpec((1, seq, K_AUG), lambda b, j: (b, 0, 0))
    return pl.pallas_call(
        functools.partial(_attn_kernel, moba=moba, n_heads=n_heads),
        out_shape=jax.ShapeDtypeStruct((batch, nb, GROUP_WIDTH, BQ), BF16),
        grid=(batch, n_steps),
        in_specs=[part_spec(0),
                  pl.BlockSpec((None, seq, width), lambda b, j: (b, 0, group * n_steps + j)),
                  part_spec(1), part_spec(2), feat_spec],
        out_specs=pl.BlockSpec((None, nb, width, BQ), lambda b, j: (b, 0, j, 0)),
        scratch_shapes=[pltpu.VMEM((n_heads, K_AUG, BQ), BF16),
                        pltpu.VMEM((n_heads, seq, K_AUG), BF16),
                        pltpu.VMEM((n_heads, V_ROWS, seq), BF16),
                        pltpu.VMEM((n_heads, seq, BQ), F32),
                        pltpu.VMEM((n_heads, seq, BQ), BF16),
                        pltpu.VMEM((n_heads, nb, 16, BQ), BF16)],
        compiler_params=pltpu.CompilerParams(
            dimension_semantics=("arbitrary", "arbitrary"), vmem_limit_bytes=VMEM_LIMIT),
        name="moba_attention" if moba else "fox_attention",
    )(t4, k3, t4, t4, feat)


def _out_proj_kernel(ya_ref, yb_ref, wa_ref, wb_ref, x_ref, g_ref, o_ref):
    for r in range(ya_ref.shape[0]):
        rows = slice(r * BQ, (r + 1) * BQ)
        y = (lax.dot_general(ya_ref[r], wa_ref[...], _TN, preferred_element_type=F32)
             + lax.dot_general(yb_ref[r], wb_ref[...], _TN, preferred_element_type=F32))
        ms = jnp.mean(y * y, axis=-1, keepdims=True)
        o_ref[rows, :] = x_ref[rows, :] + y * lax.rsqrt(ms + RMS_EPS) * g_ref[...]


def _out_proj(yt_a, yt_b, w_a, w_b, x2d, g, *, blocks_per_step=OUT_BLOCKS):
    n, d = x2d.shape
    _, nb, gw, _ = yt_a.shape
    tm = blocks_per_step * BQ
    steps_per_seq = nb // blocks_per_step
    y_spec = pl.BlockSpec((None, blocks_per_step, gw, BQ),
                          lambda i: (i // steps_per_seq, i % steps_per_seq, 0, 0))
    return pl.pallas_call(
        _out_proj_kernel,
        out_shape=jax.ShapeDtypeStruct((n, d), F32),
        grid=(n // tm,),
        in_specs=[y_spec, y_spec,
                  pl.BlockSpec((gw, d), lambda i: (0, 0)),
                  pl.BlockSpec((gw, d), lambda i: (0, 0)),
                  pl.BlockSpec((tm, d), lambda i: (i, 0)),
                  pl.BlockSpec((1, d), lambda i: (0, 0))],
        out_specs=pl.BlockSpec((tm, d), lambda i: (i, 0)),
        compiler_params=pltpu.CompilerParams(
            dimension_semantics=("arbitrary",), vmem_limit_bytes=VMEM_LIMIT),
        name="out_proj",
    )(yt_a, yt_b, w_a, w_b, x2d, g)


def _alibi_key_features(s):
    slopes = np.asarray(2.0 ** (-8.0 * np.arange(1, N_HEADS + 1) / N_HEADS), dtype=np.float32)
    bias = (slopes[:, None].astype(np.float64) * np.arange(s)[None, :] * LOG2E).astype(np.float32)

    def top_bits(v):
        return (v.view(np.uint32) & np.uint32(0xFFFF0000)).view(np.float32)

    hi = top_bits(bias)
    mid = top_bits(bias - hi)
    lo = bias - hi - mid
    feats = np.zeros((N_HEADS, s, HEAD_DIM), np.float32)
    feats[:, :, 0], feats[:, :, 1], feats[:, :, 2] = hi, mid, lo
    feats[:, :, 8:16] = (np.arange(s)[:, None] // MOBA_BLOCK == np.arange(8)[None, :])[None]
    return jnp.asarray(np.concatenate([feats, feats], axis=-1), dtype=BF16)


def _layer(x, g_pre, w_in, f_b, w_out, g_post):
    b, s, d = x.shape
    gw = GROUP_WIDTH
    scale = HEAD_DIM ** -0.5 * LOG2E
    fo = 4 * gw
    wa, wf, wb = w_in[:, :fo], w_in[:, fo:fo + N_HEADS], w_in[:, fo + N_HEADS:]

    def cols(w, part):
        return w[:, part * gw:(part + 1) * gw]

    wt = jnp.concatenate([cols(wa, 0) * scale, cols(wb, 0) * scale, cols(wa, 2), cols(wb, 2),
                          cols(wa, 3), cols(wb, 3)], axis=1).T.astype(BF16)
    pad = jnp.zeros((d, HEAD_DIM - 3 * N_HEADS), F32)
    wk = jnp.concatenate([cols(wa, 1), cols(wb, 1), wf, wf, wf, pad, wf, wf, wf, pad],
                         axis=1).astype(BF16)
    zpad = jnp.zeros((HEAD_DIM - 3 * N_HEADS,), F32)
    fb = jnp.concatenate([f_b, f_b, f_b, zpad, f_b, f_b, f_b, zpad])[None, :]

    x2d = x.reshape(b * s, d)
    t_all, k_all, flog = _in_proj(x2d, g_pre[None, :], wt, wk)
    f_parts = _fox_cumsum(flog.reshape(b, s, 128), fb)

    yt_a = _attention(t_all, k_all, f_parts, group=0, moba=False, batch=b)
    yt_b = _attention(t_all, k_all, _alibi_key_features(s), group=1, moba=True, batch=b)

    w_o = w_out.astype(BF16)
    out = _out_proj(yt_a, yt_b, w_o[:gw], w_o[gw:], x2d, g_post[None, :])
    return out.reshape(b, s, d)


def kernel(x, norm_pre_g, w_in, fox_forget_b, w_out, norm_post_g):
    for layer in range(norm_pre_g.shape[0]):
        x = _layer(x, norm_pre_g[layer], w_in[layer], fox_forget_b[layer], w_out[layer],
                   norm_post_g[layer])
    return x
```

```python
import functools

import jax
import jax.numpy as jnp
import numpy as np
from jax import lax
from jax.experimental import pallas as pl
from jax.experimental.pallas import tpu as pltpu

D_MODEL = 1024
HEAD_DIM = 64
N_HEADS = 8
GROUP_WIDTH = N_HEADS * HEAD_DIM
MOBA_BLOCK = 256
MOBA_TOPK = 3
RMS_EPS = 1e-6

BQ = 256
BK = 256
K_AUG = 128
V_ROWS = HEAD_DIM + 16
LOG2E = 1.4426950408889634
MASK_NEG = -1e30
PAIR = 2 * HEAD_DIM
HPS = 8
SUB = 128
N_SLOTS = 4
IN_BLOCKS = 2
OUT_BLOCKS = 4
F32 = jnp.float32
BF16 = jnp.bfloat16

VMEM_LIMIT = 56 * 1024 * 1024
_NT = (((1,), (1,)), ((), ()))
_TN = (((0,), (0,)), ((), ()))


def _in_proj_kernel(x_ref, g_ref, wt_ref, wk_ref, t_ref, k_ref, f_ref, *, row_chunk):
    n_k = k_ref.shape[1]
    for r in range(t_ref.shape[0]):
        x = x_ref[r * BQ:(r + 1) * BQ, :]
        ms = jnp.mean(x * x, axis=-1, keepdims=True)
        h = (x * lax.rsqrt(ms + RMS_EPS) * g_ref[...]).astype(BF16)
        for c in range(0, wt_ref.shape[0], row_chunk):
            t_ref[r, c:c + row_chunk, :] = lax.dot_general(
                wt_ref[c:c + row_chunk, :], h, _NT, preferred_element_type=F32).astype(BF16)
        kf = jnp.dot(h, wk_ref[...], preferred_element_type=F32)
        k_ref[r * BQ:(r + 1) * BQ, :] = kf[:, :n_k].astype(BF16)
        f_ref[r * BQ:(r + 1) * BQ, :] = kf[:, n_k:]


def _in_proj(x2d, g, wt, wk, *, blocks_per_step=IN_BLOCKS):
    n, d = x2d.shape
    n_t = wt.shape[0]
    n_k = wk.shape[1] - 128
    tm = blocks_per_step * BQ
    return pl.pallas_call(
        functools.partial(_in_proj_kernel, row_chunk=512),
        out_shape=(jax.ShapeDtypeStruct((n // BQ, n_t, BQ), BF16),
                   jax.ShapeDtypeStruct((n, n_k), BF16),
                   jax.ShapeDtypeStruct((n, 128), F32)),
        grid=(n // tm,),
        in_specs=[pl.BlockSpec((tm, d), lambda i: (i, 0)),
                  pl.BlockSpec((1, d), lambda i: (0, 0)),
                  pl.BlockSpec((n_t, d), lambda i: (0, 0)),
                  pl.BlockSpec((d, n_k + 128), lambda i: (0, 0))],
        out_specs=(pl.BlockSpec((blocks_per_step, n_t, BQ), lambda i: (i, 0, 0)),
                   pl.BlockSpec((tm, n_k), lambda i: (i, 0)),
                   pl.BlockSpec((tm, 128), lambda i: (i, 0))),
        compiler_params=pltpu.CompilerParams(
            dimension_semantics=("arbitrary",), vmem_limit_bytes=VMEM_LIMIT),
        name="in_proj",
    )(x2d, g, wt, wk)


def _split3(v):
    hi = v.astype(BF16)
    r = v - hi.astype(F32)
    mid = r.astype(BF16)
    lo = (r - mid.astype(F32)).astype(BF16)
    return hi, mid, lo


def _fox_cumsum_kernel(f_ref, b_ref, o_ref, *, chunk):
    s = f_ref.shape[1]
    rows = lax.broadcasted_iota(jnp.int32, (chunk, chunk), 0)
    cols = lax.broadcasted_iota(jnp.int32, (chunk, chunk), 1)
    tri = jnp.where(rows >= cols, 1.0, 0.0).astype(BF16)
    lane = lax.broadcasted_iota(jnp.int32, (chunk, 128), 1) & (HEAD_DIM - 1)
    zero = jnp.zeros((chunk, 128), BF16)

    def by_lane_group(hi, mid, lo):
        return jnp.where(lane < 8, hi, jnp.where(lane < 16, mid, jnp.where(lane < 24, lo, zero)))

    carry = jnp.zeros((1, 128), F32)
    for c in range(0, s, chunk):
        z = f_ref[0, c:c + chunk, :] + b_ref[...]
        log_f = jnp.minimum(z, 0.0) - jnp.log1p(jnp.exp(-jnp.abs(z)))
        part = jnp.dot(tri, by_lane_group(*_split3(log_f)), preferred_element_type=F32)
        total = part
        for shift in (8, 16, 128 - 8, 128 - 16):
            total = total + pltpu.roll(part, shift, axis=1)
        cum = total + carry
        carry = cum[chunk - 1:chunk, :]
        o_ref[0, c:c + chunk, :] = by_lane_group(*_split3(cum * LOG2E))


def _fox_cumsum(flog, bias_row):
    b, s, _ = flog.shape
    return pl.pallas_call(
        functools.partial(_fox_cumsum_kernel, chunk=256),
        out_shape=jax.ShapeDtypeStruct((b, s, 128), BF16),
        grid=(b,),
        in_specs=[pl.BlockSpec((1, s, 128), lambda i: (i, 0, 0)),
                  pl.BlockSpec((1, 128), lambda i: (0, 0))],
        out_specs=pl.BlockSpec((1, s, 128), lambda i: (i, 0, 0)),
        compiler_params=pltpu.CompilerParams(dimension_semantics=("arbitrary",)),
        name="fox_cumsum",
    )(flog, bias_row)


def _attn_kernel(q_ref, k_ref, v_ref, g_ref, feat_ref, o_ref,
                 qaug_ref, kaug_ref, vaug_ref, s_ref, p_ref, sel_ref, *, moba, n_heads):
    n_blocks = q_ref.shape[0]
    seq = k_ref.shape[0]
    step = pl.program_id(1)
    row64 = lax.broadcasted_iota(jnp.int32, (HEAD_DIM, BQ), 0)
    row8 = lax.broadcasted_iota(jnp.int32, (8, BQ), 0)
    row16 = lax.broadcasted_iota(jnp.int32, (16, K_AUG), 0)
    lane_k = lax.broadcasted_iota(jnp.int32, (seq, K_AUG), 1)
    key_idx = lax.broadcasted_iota(jnp.int32, (SUB, BQ), 0)
    qry_idx = lax.broadcasted_iota(jnp.int32, (SUB, BQ), 1)
    causal = [key_idx + sub * SUB <= qry_idx for sub in range(BK // SUB)]
    ones_rows = jnp.where(lax.broadcasted_iota(jnp.int32, (V_ROWS - HEAD_DIM, seq), 0) == 0,
                          1.0, 0.0).astype(BF16)

    for e in range(n_heads):
        head = n_heads * step + e
        h0 = e * HEAD_DIM
        r0, t0 = (e % 2) * HEAD_DIM, (1 - e % 2) * HEAD_DIM
        own_lanes = (lane_k >= r0) & (lane_k < r0 + HEAD_DIM)
        k_pair = k_ref[:, (e // 2) * PAIR:(e // 2 + 1) * PAIR]
        kaug_ref[e] = jnp.where(own_lanes, k_pair, feat_ref[e if moba else 0])
        vaug_ref[e, HEAD_DIM:, :] = ones_rows
        for i in range(n_blocks):
            vaug_ref[e, :HEAD_DIM, i * BQ:(i + 1) * BQ] = v_ref[i, h0:h0 + HEAD_DIM, :]
        if moba:
            tail = jnp.where(row64 < 3, 1.0, 0.0).astype(BF16)
            km = jnp.zeros((16, K_AUG), F32)
            for j in range(n_blocks):
                blk = kaug_ref[e, j * BK:(j + 1) * BK, :].astype(F32)
                km = jnp.where(row16 == j, jnp.sum(blk, axis=0, keepdims=True) * (1.0 / BK), km)
            kmean = km[:, r0:r0 + HEAD_DIM].astype(BF16)
            alibi_rows = jnp.where(row8 < 3, 1.0, 0.0)
            for i in range(n_blocks):
                gate = jnp.dot(kmean, q_ref[i, h0:h0 + HEAD_DIM, :],
                               preferred_element_type=F32)[:8, :]
                gate = jnp.where(row8 < i, gate, -jnp.inf)
                mask = jnp.zeros((8, BQ), F32)
                for j in range(i):
                    g_j = gate[j:j + 1, :]
                    beats = (gate > g_j) | ((gate == g_j) & (row8 < j))
                    rank = jnp.sum(beats.astype(F32), axis=0, keepdims=True)
                    mask = jnp.where(row8 == j, jnp.where(rank < MOBA_TOPK, 0.0, MASK_NEG), mask)
                sel_ref[e, i] = jnp.concatenate([alibi_rows, mask], axis=0).astype(BF16)
        else:
            sel = (row64 == head) | (row64 == head + 8) | (row64 == head + 16)
            tail = jnp.where(sel, -1.0, 0.0).astype(BF16)
        qaug_ref[e, t0:t0 + HEAD_DIM, :] = tail

    jobs = [(i, e) for i in range(n_blocks) for e in range(n_heads)]
    n_slots = s_ref.shape[0]
    job_max = {}

    def logits_ops(n):
        job = jobs[n]
        i, e = job
        slot = n % n_slots
        h0 = e * HEAD_DIM
        r0, t0 = (e % 2) * HEAD_DIM, (1 - e % 2) * HEAD_DIM
        state = {}

        def setup():
            qaug_ref[e, r0:r0 + HEAD_DIM, :] = q_ref[i, h0:h0 + HEAD_DIM, :]
            if moba:
                qaug_ref[e, t0:t0 + 16, :] = sel_ref[e, i]
            state["q_aug"] = qaug_ref[e]

        def tile(t):
            def run():
                if t == 0:
                    setup()
                for sub in range(BK // SUB):
                    r_lo = t * BK + sub * SUB
                    s_t = jnp.dot(kaug_ref[e, r_lo:r_lo + SUB, :], state["q_aug"],
                                  preferred_element_type=F32)
                    if t == i:
                        s_t = jnp.where(causal[sub], s_t, -jnp.inf)
                    s_ref[slot, r_lo:r_lo + SUB, :] = s_t
                    mx = jnp.max(s_t.reshape(SUB // 8, 8, BQ), axis=0)
                    state["mx"] = mx if (t == 0 and sub == 0) else jnp.maximum(state["mx"], mx)
                if t == i:
                    job_max[job] = jnp.max(state["mx"], axis=0, keepdims=True)
            return run
        return [tile(t) for t in range(i + 1)]

    def probs_ops(n):
        job = jobs[n]
        i, e = job
        slot = n % n_slots

        def tile(t):
            def run():
                for sub in range(BK // SUB):
                    rows = slice(t * BK + sub * SUB, t * BK + (sub + 1) * SUB)
                    p_ref[slot, rows, :] = jnp.exp2(s_ref[slot, rows, :] - job_max[job]).astype(BF16)
            return run
        return [tile(t) for t in range(i + 1)]

    def output_ops(n):
        job = jobs[n]
        i, e = job
        slot = n % n_slots
        h0 = e * HEAD_DIM
        state = {}

        def tile(t):
            def run():
                cols = slice(t * BK, (t + 1) * BK)
                part = jnp.dot(vaug_ref[e, :, cols], p_ref[slot, cols, :],
                               preferred_element_type=F32)
                state["acc"] = part if t == 0 else state["acc"] + part
                if t == i:
                    acc = state["acc"]
                    out = acc[:HEAD_DIM, :] / acc[HEAD_DIM:HEAD_DIM + 1, :]
                    g = g_ref[i, h0:h0 + HEAD_DIM, :].astype(F32)
                    o_ref[i, h0:h0 + HEAD_DIM, :] = (out * (g / (1.0 + jnp.exp(-g)))).astype(BF16)
            return run
        return [tile(t) for t in range(i + 1)]

    n_jobs = len(jobs)
    for stage in range(-1, n_jobs + 1):
        streams = []
        if 0 <= stage + 1 < n_jobs:
            streams.append(logits_ops(stage + 1))
        if 0 <= stage < n_jobs:
            streams.append(probs_ops(stage))
        if 0 <= stage - 1 < n_jobs:
            streams.append(output_ops(stage - 1))
        for k in range(max(len(ops) for ops in streams)):
            for ops in streams:
                if k < len(ops):
                    ops[k]()


def _attention(t_all, k_all, feat, *, group, moba, batch, n_heads=HPS):
    nb = t_all.shape[0] // batch
    seq = nb * BQ
    t4 = t_all.reshape(batch, nb, t_all.shape[1], BQ)
    k3 = k_all.reshape(batch, seq, k_all.shape[1])
    n_steps = N_HEADS // n_heads
    width = n_heads * HEAD_DIM

    def part_spec(part):
        off = (2 * part + group) * n_steps
        return pl.BlockSpec((None, nb, width, BQ), lambda b, j: (b, 0, off + j, 0))

    if moba:
        feat_spec = pl.BlockSpec((n_heads, seq, K_AUG), lambda b, j: (j, 0, 0))
    else:
        feat_spec = pl.BlockSpec((1, seq, K_AUG), lambda b, j: (b, 0, 0))
    return pl.pallas_call(
        functools.partial(_attn_kernel, moba=moba, n_heads=n_heads),
        out_shape=jax.ShapeDtypeStruct((batch, nb, GROUP_WIDTH, BQ), BF16),
        grid=(batch, n_steps),
        in_specs=[part_spec(0),
                  pl.BlockSpec((None, seq, width), lambda b, j: (b, 0, group * n_steps + j)),
                  part_spec(1), part_spec(2), feat_spec],
        out_specs=pl.BlockSpec((None, nb, width, BQ), lambda b, j: (b, 0, j, 0)),
        scratch_shapes=[pltpu.VMEM((n_heads, K_AUG, BQ), BF16),
                        pltpu.VMEM((n_heads, seq, K_AUG), BF16),
                        pltpu.VMEM((n_heads, V_ROWS, seq), BF16),
                        pltpu.VMEM((N_SLOTS, seq, BQ), F32),
                        pltpu.VMEM((N_SLOTS, seq, BQ), BF16),
                        pltpu.VMEM((n_heads, nb, 16, BQ), BF16)],
        compiler_params=pltpu.CompilerParams(
            dimension_semantics=("arbitrary", "arbitrary"), vmem_limit_bytes=VMEM_LIMIT),
        name="moba_attention" if moba else "fox_attention",
    )(t4, k3, t4, t4, feat)


def _out_proj_kernel(ya_ref, yb_ref, wa_ref, wb_ref, x_ref, g_ref, o_ref):
    for r in range(ya_ref.shape[0]):
        rows = slice(r * BQ, (r + 1) * BQ)
        y = (lax.dot_general(ya_ref[r], wa_ref[...], _TN, preferred_element_type=F32)
             + lax.dot_general(yb_ref[r], wb_ref[...], _TN, preferred_element_type=F32))
        ms = jnp.mean(y * y, axis=-1, keepdims=True)
        o_ref[rows, :] = x_ref[rows, :] + y * lax.rsqrt(ms + RMS_EPS) * g_ref[...]


def _out_proj(yt_a, yt_b, w_a, w_b, x2d, g, *, blocks_per_step=OUT_BLOCKS):
    n, d = x2d.shape
    _, nb, gw, _ = yt_a.shape
    tm = blocks_per_step * BQ
    steps_per_seq = nb // blocks_per_step
    y_spec = pl.BlockSpec((None, blocks_per_step, gw, BQ),
                          lambda i: (i // steps_per_seq, i % steps_per_seq, 0, 0))
    return pl.pallas_call(
        _out_proj_kernel,
        out_shape=jax.ShapeDtypeStruct((n, d), F32),
        grid=(n // tm,),
        in_specs=[y_spec, y_spec,
                  pl.BlockSpec((gw, d), lambda i: (0, 0)),
                  pl.BlockSpec((gw, d), lambda i: (0, 0)),
                  pl.BlockSpec((tm, d), lambda i: (i, 0)),
                  pl.BlockSpec((1, d), lambda i: (0, 0))],
        out_specs=pl.BlockSpec((tm, d), lambda i: (i, 0)),
        compiler_params=pltpu.CompilerParams(
            dimension_semantics=("arbitrary",), vmem_limit_bytes=VMEM_LIMIT),
        name="out_proj",
    )(yt_a, yt_b, w_a, w_b, x2d, g)


def _alibi_key_features(s):
    slopes = np.asarray(2.0 ** (-8.0 * np.arange(1, N_HEADS + 1) / N_HEADS), dtype=np.float32)
    bias = (slopes[:, None].astype(np.float64) * np.arange(s)[None, :] * LOG2E).astype(np.float32)

    def top_bits(v):
        return (v.view(np.uint32) & np.uint32(0xFFFF0000)).view(np.float32)

    hi = top_bits(bias)
    mid = top_bits(bias - hi)
    lo = bias - hi - mid
    feats = np.zeros((N_HEADS, s, HEAD_DIM), np.float32)
    feats[:, :, 0], feats[:, :, 1], feats[:, :, 2] = hi, mid, lo
    feats[:, :, 8:16] = (np.arange(s)[:, None] // MOBA_BLOCK == np.arange(8)[None, :])[None]
    return jnp.asarray(np.concatenate([feats, feats], axis=-1), dtype=BF16)


def _layer(x, g_pre, w_in, f_b, w_out, g_post):
    b, s, d = x.shape
    gw = GROUP_WIDTH
    scale = HEAD_DIM ** -0.5 * LOG2E
    fo = 4 * gw
    wa, wf, wb = w_in[:, :fo], w_in[:, fo:fo + N_HEADS], w_in[:, fo + N_HEADS:]

    def cols(w, part):
        return w[:, part * gw:(part + 1) * gw]

    wt = jnp.concatenate([cols(wa, 0) * scale, cols(wb, 0) * scale, cols(wa, 2), cols(wb, 2),
                          cols(wa, 3), cols(wb, 3)], axis=1).T.astype(BF16)
    pad = jnp.zeros((d, HEAD_DIM - 3 * N_HEADS), F32)
    wk = jnp.concatenate([cols(wa, 1), cols(wb, 1), wf, wf, wf, pad, wf, wf, wf, pad],
                         axis=1).astype(BF16)
    zpad = jnp.zeros((HEAD_DIM - 3 * N_HEADS,), F32)
    fb = jnp.concatenate([f_b, f_b, f_b, zpad, f_b, f_b, f_b, zpad])[None, :]

    x2d = x.reshape(b * s, d)
    t_all, k_all, flog = _in_proj(x2d, g_pre[None, :], wt, wk)
    f_parts = _fox_cumsum(flog.reshape(b, s, 128), fb)

    yt_a = _attention(t_all, k_all, f_parts, group=0, moba=False, batch=b)
    yt_b = _attention(t_all, k_all, _alibi_key_features(s), group=1, moba=True, batch=b)

    w_o = w_out.astype(BF16)
    out = _out_proj(yt_a, yt_b, w_o[:gw], w_o[gw:], x2d, g_post[None, :])
    return out.reshape(b, s, d)


def kernel(x, norm_pre_g, w_in, fox_forget_b, w_out, norm_post_g):
    for layer in range(norm_pre_g.shape[0]):
        x = _layer(x, norm_pre_g[layer], w_in[layer], fox_forget_b[layer], w_out[layer],
                   norm_post_g[layer])
    return x
```

```python
import functools

import jax
import jax.numpy as jnp
import numpy as np
from jax import lax
from jax.experimental import pallas as pl
from jax.experimental.pallas import tpu as pltpu

D_MODEL = 1024
HEAD_DIM = 64
N_HEADS = 8
GROUP_WIDTH = N_HEADS * HEAD_DIM
MOBA_BLOCK = 256
MOBA_TOPK = 3
RMS_EPS = 1e-6

BQ = 256
BK = 256
K_AUG = 128
V_ROWS = HEAD_DIM + 16
LOG2E = 1.4426950408889634
MASK_NEG = -1e30
PAIR = 2 * HEAD_DIM
HPS = 4
SUB = 128
N_SLOTS = 4
IN_BLOCKS = 4
OUT_BLOCKS = 8
F32 = jnp.float32
BF16 = jnp.bfloat16

VMEM_LIMIT = 56 * 1024 * 1024
_NT = (((1,), (1,)), ((), ()))
_TN = (((0,), (0,)), ((), ()))


def _in_proj_kernel(x_ref, g_ref, wt_ref, wk_ref, t_ref, k_ref, f_ref, *, row_chunk):
    n_k = k_ref.shape[1]
    for r in range(t_ref.shape[0]):
        x = x_ref[r * BQ:(r + 1) * BQ, :]
        ms = jnp.mean(x * x, axis=-1, keepdims=True)
        h = (x * lax.rsqrt(ms + RMS_EPS) * g_ref[...]).astype(BF16)
        for c in range(0, wt_ref.shape[0], row_chunk):
            t_ref[r, c:c + row_chunk, :] = lax.dot_general(
                wt_ref[c:c + row_chunk, :], h, _NT, preferred_element_type=F32).astype(BF16)
        kf = jnp.dot(h, wk_ref[...], preferred_element_type=F32)
        k_ref[r * BQ:(r + 1) * BQ, :] = kf[:, :n_k].astype(BF16)
        f_ref[r * BQ:(r + 1) * BQ, :] = kf[:, n_k:]


def _in_proj(x2d, g, wt, wk, *, blocks_per_step=IN_BLOCKS):
    n, d = x2d.shape
    n_t = wt.shape[0]
    n_k = wk.shape[1] - 128
    tm = blocks_per_step * BQ
    return pl.pallas_call(
        functools.partial(_in_proj_kernel, row_chunk=512),
        out_shape=(jax.ShapeDtypeStruct((n // BQ, n_t, BQ), BF16),
                   jax.ShapeDtypeStruct((n, n_k), BF16),
                   jax.ShapeDtypeStruct((n, 128), F32)),
        grid=(n // tm,),
        in_specs=[pl.BlockSpec((tm, d), lambda i: (i, 0)),
                  pl.BlockSpec((1, d), lambda i: (0, 0)),
                  pl.BlockSpec((n_t, d), lambda i: (0, 0)),
                  pl.BlockSpec((d, n_k + 128), lambda i: (0, 0))],
        out_specs=(pl.BlockSpec((blocks_per_step, n_t, BQ), lambda i: (i, 0, 0)),
                   pl.BlockSpec((tm, n_k), lambda i: (i, 0)),
                   pl.BlockSpec((tm, 128), lambda i: (i, 0))),
        compiler_params=pltpu.CompilerParams(
            dimension_semantics=("arbitrary",), vmem_limit_bytes=VMEM_LIMIT),
        name="in_proj",
    )(x2d, g, wt, wk)


def _split3(v):
    hi = v.astype(BF16)
    r = v - hi.astype(F32)
    mid = r.astype(BF16)
    lo = (r - mid.astype(F32)).astype(BF16)
    return hi, mid, lo


def _fox_cumsum_kernel(f_ref, b_ref, o_ref, *, chunk):
    s = f_ref.shape[1]
    rows = lax.broadcasted_iota(jnp.int32, (chunk, chunk), 0)
    cols = lax.broadcasted_iota(jnp.int32, (chunk, chunk), 1)
    tri = jnp.where(rows >= cols, 1.0, 0.0).astype(BF16)
    lane = lax.broadcasted_iota(jnp.int32, (chunk, 128), 1) & (HEAD_DIM - 1)
    zero = jnp.zeros((chunk, 128), BF16)

    def by_lane_group(hi, mid, lo):
        return jnp.where(lane < 8, hi, jnp.where(lane < 16, mid, jnp.where(lane < 24, lo, zero)))

    carry = jnp.zeros((1, 128), F32)
    for c in range(0, s, chunk):
        z = f_ref[0, c:c + chunk, :] + b_ref[...]
        log_f = jnp.minimum(z, 0.0) - jnp.log1p(jnp.exp(-jnp.abs(z)))
        part = jnp.dot(tri, by_lane_group(*_split3(log_f)), preferred_element_type=F32)
        total = part
        for shift in (8, 16, 128 - 8, 128 - 16):
            total = total + pltpu.roll(part, shift, axis=1)
        cum = total + carry
        carry = cum[chunk - 1:chunk, :]
        o_ref[0, c:c + chunk, :] = by_lane_group(*_split3(cum * LOG2E))


def _fox_cumsum(flog, bias_row):
    b, s, _ = flog.shape
    return pl.pallas_call(
        functools.partial(_fox_cumsum_kernel, chunk=256),
        out_shape=jax.ShapeDtypeStruct((b, s, 128), BF16),
        grid=(b,),
        in_specs=[pl.BlockSpec((1, s, 128), lambda i: (i, 0, 0)),
                  pl.BlockSpec((1, 128), lambda i: (0, 0))],
        out_specs=pl.BlockSpec((1, s, 128), lambda i: (i, 0, 0)),
        compiler_params=pltpu.CompilerParams(dimension_semantics=("arbitrary",)),
        name="fox_cumsum",
    )(flog, bias_row)


def _attn_kernel(q_ref, k_ref, v_ref, g_ref, feat_ref, o_ref,
                 qaug_ref, kaug_ref, vaug_ref, s_ref, p_ref, sel_ref, *, moba, n_heads):
    n_blocks = q_ref.shape[0]
    seq = k_ref.shape[0]
    step = pl.program_id(1)
    row64 = lax.broadcasted_iota(jnp.int32, (HEAD_DIM, BQ), 0)
    row8 = lax.broadcasted_iota(jnp.int32, (8, BQ), 0)
    row16 = lax.broadcasted_iota(jnp.int32, (16, K_AUG), 0)
    lane_k = lax.broadcasted_iota(jnp.int32, (seq, K_AUG), 1)
    key_idx = lax.broadcasted_iota(jnp.int32, (SUB, BQ), 0)
    qry_idx = lax.broadcasted_iota(jnp.int32, (SUB, BQ), 1)
    causal = [key_idx + sub * SUB <= qry_idx for sub in range(BK // SUB)]
    ones_rows = jnp.where(lax.broadcasted_iota(jnp.int32, (V_ROWS - HEAD_DIM, seq), 0) == 0,
                          1.0, 0.0).astype(BF16)

    for e in range(n_heads):
        head = n_heads * step + e
        h0 = e * HEAD_DIM
        r0, t0 = (e % 2) * HEAD_DIM, (1 - e % 2) * HEAD_DIM
        own_lanes = (lane_k >= r0) & (lane_k < r0 + HEAD_DIM)
        k_pair = k_ref[:, (e // 2) * PAIR:(e // 2 + 1) * PAIR]
        kaug_ref[e] = jnp.where(own_lanes, k_pair, feat_ref[e if moba else 0])
        vaug_ref[e, HEAD_DIM:, :] = ones_rows
        for i in range(n_blocks):
            vaug_ref[e, :HEAD_DIM, i * BQ:(i + 1) * BQ] = v_ref[i, h0:h0 + HEAD_DIM, :]
        if moba:
            tail = jnp.where(row64 < 3, 1.0, 0.0).astype(BF16)
            km = jnp.zeros((16, K_AUG), F32)
            for j in range(n_blocks):
                blk = kaug_ref[e, j * BK:(j + 1) * BK, :].astype(F32)
                km = jnp.where(row16 == j, jnp.sum(blk, axis=0, keepdims=True) * (1.0 / BK), km)
            kmean = km[:, r0:r0 + HEAD_DIM].astype(BF16)
            alibi_rows = jnp.where(row8 < 3, 1.0, 0.0)
            for i in range(n_blocks):
                gate = jnp.dot(kmean, q_ref[i, h0:h0 + HEAD_DIM, :],
                               preferred_element_type=F32)[:8, :]
                gate = jnp.where(row8 < i, gate, -jnp.inf)
                mask = jnp.zeros((8, BQ), F32)
                for j in range(i):
                    g_j = gate[j:j + 1, :]
                    beats = (gate > g_j) | ((gate == g_j) & (row8 < j))
                    rank = jnp.sum(beats.astype(F32), axis=0, keepdims=True)
                    mask = jnp.where(row8 == j, jnp.where(rank < MOBA_TOPK, 0.0, MASK_NEG), mask)
                sel_ref[e, i] = jnp.concatenate([alibi_rows, mask], axis=0).astype(BF16)
        else:
            sel = (row64 == head) | (row64 == head + 8) | (row64 == head + 16)
            tail = jnp.where(sel, -1.0, 0.0).astype(BF16)
        qaug_ref[e, t0:t0 + HEAD_DIM, :] = tail

    jobs = [(i, e) for i in range(n_blocks) for e in range(n_heads)]
    n_slots = s_ref.shape[0]
    job_max = {}

    def logits_ops(n):
        job = jobs[n]
        i, e = job
        slot = n % n_slots
        h0 = e * HEAD_DIM
        r0, t0 = (e % 2) * HEAD_DIM, (1 - e % 2) * HEAD_DIM
        state = {}

        def setup():
            qaug_ref[e, r0:r0 + HEAD_DIM, :] = q_ref[i, h0:h0 + HEAD_DIM, :]
            if moba:
                qaug_ref[e, t0:t0 + 16, :] = sel_ref[e, i]
            state["q_aug"] = qaug_ref[e]

        def tile(t):
            def run():
                if t == 0:
                    setup()
                for sub in range(BK // SUB):
                    r_lo = t * BK + sub * SUB
                    s_t = jnp.dot(kaug_ref[e, r_lo:r_lo + SUB, :], state["q_aug"],
                                  preferred_element_type=F32)
                    if t == i:
                        s_t = jnp.where(causal[sub], s_t, -jnp.inf)
                    s_ref[slot, r_lo:r_lo + SUB, :] = s_t
                    mx = jnp.max(s_t.reshape(SUB // 8, 8, BQ), axis=0)
                    state["mx"] = mx if (t == 0 and sub == 0) else jnp.maximum(state["mx"], mx)
                if t == i:
                    job_max[job] = jnp.max(state["mx"], axis=0, keepdims=True)
            return run
        return [tile(t) for t in range(i + 1)]

    def probs_ops(n):
        job = jobs[n]
        i, e = job
        slot = n % n_slots

        def tile(t):
            def run():
                for sub in range(BK // SUB):
                    rows = slice(t * BK + sub * SUB, t * BK + (sub + 1) * SUB)
                    p_ref[slot, rows, :] = jnp.exp2(s_ref[slot, rows, :] - job_max[job]).astype(BF16)
            return run
        return [tile(t) for t in range(i + 1)]

    def output_ops(n):
        job = jobs[n]
        i, e = job
        slot = n % n_slots
        h0 = e * HEAD_DIM
        state = {}

        def tile(t):
            def run():
                cols = slice(t * BK, (t + 1) * BK)
                part = jnp.dot(vaug_ref[e, :, cols], p_ref[slot, cols, :],
                               preferred_element_type=F32)
                state["acc"] = part if t == 0 else state["acc"] + part
                if t == i:
                    acc = state["acc"]
                    out = acc[:HEAD_DIM, :] / acc[HEAD_DIM:HEAD_DIM + 1, :]
                    g = g_ref[i, h0:h0 + HEAD_DIM, :].astype(F32)
                    o_ref[i, h0:h0 + HEAD_DIM, :] = (out * (g / (1.0 + jnp.exp(-g)))).astype(BF16)
            return run
        return [tile(t) for t in range(i + 1)]

    n_jobs = len(jobs)
    for stage in range(-1, n_jobs + 1):
        streams = []
        if 0 <= stage + 1 < n_jobs:
            streams.append(logits_ops(stage + 1))
        if 0 <= stage < n_jobs:
            streams.append(probs_ops(stage))
        if 0 <= stage - 1 < n_jobs:
            streams.append(output_ops(stage - 1))
        for k in range(max(len(ops) for ops in streams)):
            for ops in streams:
                if k < len(ops):
                    ops[k]()


def _attention(t_all, k_all, feat, *, group, moba, batch, n_heads=HPS):
    nb = t_all.shape[0] // batch
    seq = nb * BQ
    t4 = t_all.reshape(batch, nb, t_all.shape[1], BQ)
    k3 = k_all.reshape(batch, seq, k_all.shape[1])
    n_steps = N_HEADS // n_heads
    width = n_heads * HEAD_DIM

    def part_spec(part):
        off = (2 * part + group) * n_steps
        return pl.BlockSpec((None, nb, width, BQ), lambda b, j: (b, 0, off + j, 0))

    if moba:
        feat_spec = pl.BlockSpec((n_heads, seq, K_AUG), lambda b, j: (j, 0, 0))
    else:
        feat_spec = pl.BlockSpec((1, seq, K_AUG), lambda b, j: (b, 0, 0))
    return pl.pallas_call(
        functools.partial(_attn_kernel, moba=moba, n_heads=n_heads),
        out_shape=jax.ShapeDtypeStruct((batch, nb, GROUP_WIDTH, BQ), BF16),
        grid=(batch, n_steps),
        in_specs=[part_spec(0),
                  pl.BlockSpec((None, seq, width), lambda b, j: (b, 0, group * n_steps + j)),
                  part_spec(1), part_spec(2), feat_spec],
        out_specs=pl.BlockSpec((None, nb, width, BQ), lambda b, j: (b, 0, j, 0)),
        scratch_shapes=[pltpu.VMEM((n_heads, K_AUG, BQ), BF16),
                        pltpu.VMEM((n_heads, seq, K_AUG), BF16),
                        pltpu.VMEM((n_heads, V_ROWS, seq), BF16),
                        pltpu.VMEM((N_SLOTS, seq, BQ), F32),
                        pltpu.VMEM((N_SLOTS, seq, BQ), BF16),
                        pltpu.VMEM((n_heads, nb, 16, BQ), BF16)],
        compiler_params=pltpu.CompilerParams(
            dimension_semantics=("arbitrary", "arbitrary"), vmem_limit_bytes=VMEM_LIMIT),
        name="moba_attention" if moba else "fox_attention",
    )(t4, k3, t4, t4, feat)


def _out_proj_kernel(ya_ref, yb_ref, wa_ref, wb_ref, x_ref, g_ref, o_ref):
    for r in range(ya_ref.shape[0]):
        rows = slice(r * BQ, (r + 1) * BQ)
        y = (lax.dot_general(ya_ref[r], wa_ref[...], _TN, preferred_element_type=F32)
             + lax.dot_general(yb_ref[r], wb_ref[...], _TN, preferred_element_type=F32))
        ms = jnp.mean(y * y, axis=-1, keepdims=True)
        o_ref[rows, :] = x_ref[rows, :] + y * lax.rsqrt(ms + RMS_EPS) * g_ref[...]


def _out_proj(yt_a, yt_b, w_a, w_b, x2d, g, *, blocks_per_step=OUT_BLOCKS):
    n, d = x2d.shape
    _, nb, gw, _ = yt_a.shape
    tm = blocks_per_step * BQ
    steps_per_seq = nb // blocks_per_step
    y_spec = pl.BlockSpec((None, blocks_per_step, gw, BQ),
                          lambda i: (i // steps_per_seq, i % steps_per_seq, 0, 0))
    return pl.pallas_call(
        _out_proj_kernel,
        out_shape=jax.ShapeDtypeStruct((n, d), F32),
        grid=(n // tm,),
        in_specs=[y_spec, y_spec,
                  pl.BlockSpec((gw, d), lambda i: (0, 0)),
                  pl.BlockSpec((gw, d), lambda i: (0, 0)),
                  pl.BlockSpec((tm, d), lambda i: (i, 0)),
                  pl.BlockSpec((1, d), lambda i: (0, 0))],
        out_specs=pl.BlockSpec((tm, d), lambda i: (i, 0)),
        compiler_params=pltpu.CompilerParams(
            dimension_semantics=("arbitrary",), vmem_limit_bytes=VMEM_LIMIT),
        name="out_proj",
    )(yt_a, yt_b, w_a, w_b, x2d, g)


def _alibi_key_features(s):
    slopes = np.asarray(2.0 ** (-8.0 * np.arange(1, N_HEADS + 1) / N_HEADS), dtype=np.float32)
    bias = (slopes[:, None].astype(np.float64) * np.arange(s)[None, :] * LOG2E).astype(np.float32)

    def top_bits(v):
        return (v.view(np.uint32) & np.uint32(0xFFFF0000)).view(np.float32)

    hi = top_bits(bias)
    mid = top_bits(bias - hi)
    lo = bias - hi - mid
    feats = np.zeros((N_HEADS, s, HEAD_DIM), np.float32)
    feats[:, :, 0], feats[:, :, 1], feats[:, :, 2] = hi, mid, lo
    feats[:, :, 8:16] = (np.arange(s)[:, None] // MOBA_BLOCK == np.arange(8)[None, :])[None]
    return jnp.asarray(np.concatenate([feats, feats], axis=-1), dtype=BF16)


def _layer(x, g_pre, w_in, f_b, w_out, g_post):
    b, s, d = x.shape
    gw = GROUP_WIDTH
    scale = HEAD_DIM ** -0.5 * LOG2E
    fo = 4 * gw
    col_scale = np.ones((w_in.shape[1],), np.float32)
    col_scale[:gw] = scale
    col_scale[fo + N_HEADS:fo + N_HEADS + gw] = scale
    w16 = (w_in * col_scale[None, :]).astype(BF16)
    wa, wf, wb = w16[:, :fo], w16[:, fo:fo + N_HEADS], w16[:, fo + N_HEADS:]

    def cols(w, part):
        return w[:, part * gw:(part + 1) * gw]

    wt = jnp.concatenate([cols(wa, 0), cols(wb, 0), cols(wa, 2), cols(wb, 2),
                          cols(wa, 3), cols(wb, 3)], axis=1).T
    pad = jnp.zeros((d, HEAD_DIM - 3 * N_HEADS), BF16)
    wk = jnp.concatenate([cols(wa, 1), cols(wb, 1), wf, wf, wf, pad, wf, wf, wf, pad], axis=1)
    zpad = jnp.zeros((HEAD_DIM - 3 * N_HEADS,), F32)
    fb = jnp.concatenate([f_b, f_b, f_b, zpad, f_b, f_b, f_b, zpad])[None, :]

    x2d = x.reshape(b * s, d)
    t_all, k_all, flog = _in_proj(x2d, g_pre[None, :], wt, wk)
    f_parts = _fox_cumsum(flog.reshape(b, s, 128), fb)

    yt_a = _attention(t_all, k_all, f_parts, group=0, moba=False, batch=b)
    yt_b = _attention(t_all, k_all, _alibi_key_features(s), group=1, moba=True, batch=b)

    w_o = w_out.astype(BF16)
    out = _out_proj(yt_a, yt_b, w_o[:gw], w_o[gw:], x2d, g_post[None, :])
    return out.reshape(b, s, d)


def kernel(x, norm_pre_g, w_in, fox_forget_b, w_out, norm_post_g):
    for layer in range(norm_pre_g.shape[0]):
        x = _layer(x, norm_pre_g[layer], w_in[layer], fox_forget_b[layer], w_out[layer],
                   norm_post_g[layer])
    return x
```
